```python
import jax
import jax.numpy as jnp
from jax import lax
import numpy as np

D_MODEL = 1024
BATCH = 32
SEQ = 2048
DEPTH = 4
DEC_BATCH = 16
DEC_SEQ = 4096
PAST_LEN = 128

GRID_W = 64
HEAD_DIM = 64
BRANCH_W = D_MODEL // 4
N_BRANCH = 3
NA_HEADS = BRANCH_W // HEAD_DIM
NA_W = NA_HEADS * HEAD_DIM
NA_WIN_ROWS = 8
NA_WIN_COLS = 16
NA_COL_BLOCK = 16
NA_COL_BAND = NA_COL_BLOCK + NA_WIN_COLS
ML_HEADS = BRANCH_W // HEAD_DIM
ML_W = ML_HEADS * HEAD_DIM
ML_CHUNK = 64
ML_CONV = 3
ML_NORM_EPS = 1e-6
RW_HEADS = BRANCH_W // HEAD_DIM
RW_W = RW_HEADS * HEAD_DIM
RW_CONV = 3
RW_DECAY_RANK = 32
RW_AAA_RANK = 32
RW_GATE_RANK = 64
RW_DECAY_SCALE = 0.606531
RW_NORM_EPS = 64e-5
N_EXPERTS = 16
N_GROUPS = 4
EXPERTS_PER_GROUP = N_EXPERTS // N_GROUPS
TOP_K = 2
D_EXPERT = 512
ALPHA = (2 * DEPTH) ** 0.25
BETA = (8 * DEPTH) ** -0.25
LN_EPS = 1e-5
NEG = -1e30
IN_SIZES = (3 * NA_W, 3 * ML_W, ML_W, 4 * ML_HEADS, 3 * RW_W, 2 * RW_DECAY_RANK, 2 * RW_AAA_RANK, RW_GATE_RANK, N_BRANCH * D_MODEL)
D_IN = sum(IN_SIZES)

kernel_name = "hybrid_natten_mlstm_rwkv7_grouped_moe_encoder"


def _split_points(sizes):
    pts, acc = [], 0
    for s in sizes[:-1]:
        acc += s
        pts.append(acc)
    return pts


def _layernorm(x, g, b):
    xf = x.astype(jnp.float32)
    mu = jnp.mean(xf, axis=-1, keepdims=True)
    var = jnp.mean(jnp.square(xf - mu), axis=-1, keepdims=True)
    return ((xf - mu) * lax.rsqrt(var + LN_EPS) * g + b).astype(x.dtype)


def _head_norm(h, eps):
    mu = jnp.mean(h, axis=-1, keepdims=True)
    var = jnp.mean(jnp.square(h - mu), axis=-1, keepdims=True)
    return (h - mu) * lax.rsqrt(var + eps)


def _dwconv_centred(x, w):
    return lax.conv_general_dilated(x, w[:, None, :].astype(x.dtype), window_strides=(1,), padding="SAME",
                                    dimension_numbers=("NWC", "WIO", "NWC"), feature_group_count=x.shape[-1])


def _neighbourhood_attention(qkv, rpb):
    B, S, _ = qkv.shape
    rows = S // GRID_W
    kr = min(NA_WIN_ROWS, rows)
    ncb = GRID_W // NA_COL_BLOCK
    q, k, v = jnp.split(qkv, 3, axis=-1)
    q = (q * HEAD_DIM ** -0.5).reshape(B, rows, ncb, NA_COL_BLOCK, NA_HEADS, HEAD_DIM)
    k = k.reshape(B, rows, GRID_W, NA_HEADS, HEAD_DIM)
    v = v.reshape(B, rows, GRID_W, NA_HEADS, HEAD_DIM)
    cols = np.arange(GRID_W)
    q_cols = cols.reshape(ncb, NA_COL_BLOCK)
    win_start = np.clip(q_cols - NA_WIN_COLS // 2, 0, GRID_W - NA_WIN_COLS)
    band_start = np.clip(q_cols[:, 0] - NA_WIN_COLS // 2, 0, GRID_W - NA_COL_BAND)
    k_cols = band_start[:, None] + np.arange(NA_COL_BAND)[None, :]
    kc = k_cols[:, None, :]
    col_ok = (kc >= win_start[:, :, None]) & (kc < win_start[:, :, None] + NA_WIN_COLS)
    col_idx = np.clip(kc - q_cols[:, :, None] + NA_WIN_COLS - 1, 0, 2 * NA_WIN_COLS - 2)
    bias_c = jnp.transpose(rpb[:, :, col_idx], (0, 2, 3, 1, 4)).astype(jnp.float32)
    k_band = k[:, :, k_cols]
    v_band = v[:, :, k_cols]
    mask = col_ok[None, None, :, :, None, :]

    def row_block(args):
        r, q_r = args
        rs = jnp.clip(r - kr // 2, 0, rows - kr)
        k_r = lax.dynamic_slice_in_dim(k_band, rs, kr, axis=1)
        v_r = lax.dynamic_slice_in_dim(v_band, rs, kr, axis=1)
        row_idx = rs + jnp.arange(kr) - r + NA_WIN_ROWS - 1
        bias = jnp.take(bias_c, row_idx, axis=3)
        s = jnp.einsum("bnqhd,binjhd->bhnqij", q_r, k_r).astype(jnp.float32) + bias
        s = jnp.where(mask, s, NEG)
        p = jax.nn.softmax(s.reshape(s.shape[:4] + (kr * NA_COL_BAND,)), axis=-1).reshape(s.shape)
        return jnp.einsum("bhnqij,binjhd->bnqhd", p.astype(v_r.dtype), v_r)

    out = lax.map(row_block, (jnp.arange(rows), jnp.moveaxis(q, 1, 0)))
    return jnp.moveaxis(out, 0, 1).reshape(B, S, NA_W)


def _mlstm_chunkwise(q, k, v, ig, lf):
    B, H, S, d = q.shape
    L = ML_CHUNK
    nch = S // L
    q = q.reshape(B, H, nch, L, d)
    k = k.reshape(B, H, nch, L, d)
    v = v.reshape(B, H, nch, L, d)
    ig = ig.reshape(B, H, nch, L)
    lf = lf.reshape(B, H, nch, L)
    b = jnp.cumsum(lf, axis=-1)
    b_last = b[..., -1]
    tri = np.tril(np.ones((L, L), dtype=bool))
    dmat = jnp.where(tri, b[..., :, None] - b[..., None, :] + ig[..., None, :], NEG)
    m_intra = jnp.max(dmat, axis=-1)
    a = b_last[..., None] - b + ig
    m_chunk = jnp.max(a, axis=-1)
    wa = jnp.exp(a - m_chunk[..., None])
    c_chunk = jnp.einsum("bhcl,bhclv,bhclk->bhcvk", wa, v, k)
    n_chunk = jnp.einsum("bhcl,bhclk->bhck", wa, k)

    def step(carry, inp):
        c_st, n_st, m_st = carry
        c_c, n_c, m_c, bl = inp
        m_new = jnp.maximum(bl + m_st, m_c)
        s_prev = jnp.exp(bl + m_st - m_new)
        s_cur = jnp.exp(m_c - m_new)
        c_new = s_prev[..., None, None] * c_st + s_cur[..., None, None] * c_c
        n_new = s_prev[..., None] * n_st + s_cur[..., None] * n_c
        return (c_new, n_new, m_new), (c_st, n_st, m_st)

    init = (jnp.zeros((B, H, d, d), jnp.float32), jnp.zeros((B, H, d), jnp.float32), jnp.full((B, H), NEG, jnp.float32))
    xs = (jnp.moveaxis(c_chunk, 2, 0), jnp.moveaxis(n_chunk, 2, 0), jnp.moveaxis(m_chunk, 2, 0), jnp.moveaxis(b_last, 2, 0))
    _, (c_prev, n_prev, m_prev) = lax.scan(step, init, xs)
    c_prev = jnp.moveaxis(c_prev, 0, 2)
    n_prev = jnp.moveaxis(n_prev, 0, 2)
    m_prev = jnp.moveaxis(m_prev, 0, 2)
    g = b + m_prev[..., None]
    m_t = jnp.maximum(g, m_intra)
    inter = jnp.exp(g - m_t)
    w_intra = jnp.einsum("bhcld,bhcsd->bhcls", q, k) * jnp.exp(dmat - m_t[..., None])
    num = inter[..., None] * jnp.einsum("bhcvk,bhclk->bhclv", c_prev, q) + jnp.einsum("bhcls,bhcsv->bhclv", w_intra, v)
    den = inter * jnp.einsum("bhck,bhclk->bhcl", n_prev, q) + jnp.sum(w_intra, axis=-1)
    h = num / jnp.maximum(jnp.abs(den), jnp.exp(-m_t))[..., None]
    return h.reshape(B, H, S, d)


def _mlstm_mixer(qkv, o_pre, gate_pre, conv_w, gate_b, norm_g):
    B, S, _ = qkv.shape
    qk = jax.nn.silu(_dwconv_centred(qkv[..., :2 * ML_W], conv_w))
    q, k = jnp.split(qk, 2, axis=-1)
    v = qkv[..., 2 * ML_W:]

    def heads(t):
        return t.reshape(B, S, ML_HEADS, HEAD_DIM).transpose(0, 2, 1, 3).astype(jnp.float32)

    q, k, v = heads(q), heads(k) * HEAD_DIM ** -0.5, heads(v)
    gates = (gate_pre + gate_b).astype(jnp.float32).reshape(B, S, 2, 2, ML_HEADS).transpose(2, 3, 0, 4, 1)
    ig = gates[:, 0]
    lf = jax.nn.log_sigmoid(gates[:, 1])
    h_fwd = _mlstm_chunkwise(q, k, v, ig[0], lf[0])
    h_bwd = jnp.flip(_mlstm_chunkwise(jnp.flip(q, 2), jnp.flip(k, 2), jnp.flip(v, 2), jnp.flip(ig[1], 2), jnp.flip(lf[1], 2)), 2)
    h = _head_norm(h_fwd + h_bwd, ML_NORM_EPS).transpose(0, 2, 1, 3).reshape(B, S, ML_W) * norm_g
    return (h * jax.nn.sigmoid(o_pre.astype(jnp.float32))).astype(qkv.dtype)


def _rwkv7_scan(r, w, k, v, kk, a, reverse):
    B, H, d = r.shape[1:]

    def step(st, inp):
        r_t, w_t, k_t, v_t, kk_t, a_t = inp
        st_kk = jnp.einsum("bhvk,bhk->bhv", st, kk_t)
        st = st * w_t[:, :, None, :] - st_kk[..., None] * (kk_t * a_t)[:, :, None, :] + v_t[..., None] * k_t[:, :, None, :]
        return st, jnp.einsum("bhvk,bhk->bhv", st, r_t)

    _, y = lax.scan(step, jnp.zeros((B, H, d, d), jnp.float32), (r, w, k, v, kk, a), reverse=reverse)
    return y


def _rwkv7_mixer(rkv, w_lo, a_lo, g_lo, conv_w, w0, w_up, a0, a_up, g_up, k_k, k_a, r_k, norm_g, norm_b):
    B, S, _ = rkv.shape
    r, k, v = jnp.split(_dwconv_centred(rkv, conv_w).astype(jnp.float32), 3, axis=-1)
    w_lo = w_lo.astype(jnp.float32).reshape(B, S, 2, RW_DECAY_RANK)
    a_lo = a_lo.astype(jnp.float32).reshape(B, S, 2, RW_AAA_RANK)
    decay = jnp.exp(-RW_DECAY_SCALE * jax.nn.sigmoid(w0 + jnp.einsum("bsdr,drc->bsdc", jnp.tanh(w_lo), w_up)))
    a = jax.nn.sigmoid(a0 + jnp.einsum("bsdr,drc->bsdc", a_lo, a_up))
    g = jax.nn.sigmoid(g_lo.astype(jnp.float32)) @ g_up
    k_dir = k[:, :, None, :] * (1.0 + (a - 1.0) * k_a)

    def hd(t):
        return t.reshape(B, S, RW_HEADS, HEAD_DIM)

    def tm(t):
        return jnp.moveaxis(t, 1, 0)

    r_h, v_h = hd(r), hd(v)
    kk = hd(k * k_k)
    kk = kk / jnp.maximum(jnp.sqrt(jnp.sum(jnp.square(kk), axis=-1, keepdims=True)), 1e-12)
    y_fwd = _rwkv7_scan(tm(r_h), tm(hd(decay[:, :, 0])), tm(hd(k_dir[:, :, 0])), tm(v_h), tm(kk), tm(hd(a[:, :, 0])), False)
    y_bwd = _rwkv7_scan(tm(r_h), tm(hd(decay[:, :, 1])), tm(hd(k_dir[:, :, 1])), tm(v_h), tm(kk), tm(hd(a[:, :, 1])), True)
    y = _head_norm(jnp.moveaxis(y_fwd + y_bwd, 0, 1), RW_NORM_EPS).reshape(B, S, RW_W) * norm_g + norm_b
    bonus = jnp.sum(r_h[:, :, None] * k_dir.reshape(B, S, 2, RW_HEADS, HEAD_DIM) * r_k, axis=(2, 4))[..., None] * v_h
    return ((y + bonus.reshape(B, S, RW_W)) * g).astype(rkv.dtype)


def _grouped_moe(x, w_router, router_bias, w1, w3, w2):
    B, S, D = x.shape
    xt = x.reshape(B * S, D)
    probs = jax.nn.softmax((xt @ w_router).astype(jnp.float32), axis=-1)
    sel = probs + router_bias.astype(jnp.float32)
    group_score = jnp.sum(lax.top_k(sel.reshape(-1, N_GROUPS, EXPERTS_PER_GROUP), TOP_K)[0], axis=-1)
    group = jnp.argmax(group_score, axis=-1)
    in_group = (np.arange(N_EXPERTS) // EXPERTS_PER_GROUP)[None, :] == group[:, None]
    _, idx = lax.top_k(jnp.where(in_group, sel, -jnp.inf), TOP_K)
    w_sel = jnp.take_along_axis(probs, idx, axis=-1)
    w_sel = w_sel / jnp.sum(w_sel, axis=-1, keepdims=True)
    gates = jnp.einsum("nke,nk->ne", jax.nn.one_hot(idx, N_EXPERTS, dtype=jnp.float32), w_sel).astype(x.dtype)
    y = jnp.zeros_like(xt)
    for e in range(N_EXPERTS):
        h = jax.nn.silu(xt @ w1[e]) * (xt @ w3[e])
        y = y + gates[:, e:e + 1] * (h @ w2[e])
    return y.reshape(B, S, D)


def _mixer_sublayer(x, l, p):
    u = x @ p["w_in"][l]
    na_qkv, ml_qkv, ml_o, ml_g, rw_rkv, rw_wl, rw_al, rw_gl, merge = jnp.split(u, _split_points(IN_SIZES), axis=-1)
    o_a = _neighbourhood_attention(na_qkv, p["na_rpb"][l])
    o_b = _mlstm_mixer(ml_qkv, ml_o, ml_g, p["ml_conv"][l], p["ml_gate_b"][l], p["ml_norm_g"][l])
    o_c = _rwkv7_mixer(rw_rkv, rw_wl, rw_al, rw_gl, p["rw_conv"][l], p["rw_w0"][l], p["rw_w_up"][l], p["rw_a0"][l],
                       p["rw_a_up"][l], p["rw_g_up"][l], p["rw_k_k"][l], p["rw_k_a"][l], p["rw_r_k"][l],
                       p["rw_norm_g"][l], p["rw_norm_b"][l])
    g_a, g_b, g_c = jnp.split(jax.nn.sigmoid(merge), N_BRANCH, axis=-1)
    mixed = g_a * (o_a @ p["w_br_a"][l]) + g_b * (o_b @ p["w_br_b"][l]) + g_c * (o_c @ p["w_br_c"][l])
    return _layernorm(ALPHA * x + mixed @ p["w_out"][l], p["ln1_g"][l], p["ln1_b"][l])


def _moe_sublayer(x, l, p):
    y = _grouped_moe(x, p["w_router"], p["router_bias"], p["moe_w1"][l], p["moe_w3"][l], p["moe_w2"][l])
    return _layernorm(ALPHA * x + y, p["ln2_g"][l], p["ln2_b"][l])


def _trunk(x, p):
    x = _layernorm(x, p["ln0_g"], p["ln0_b"])
    for l in range(DEPTH):
        x = _mixer_sublayer(x, l, p)
        x = _moe_sublayer(x, l, p)
    return x


def setup_inputs(seed: int = 0) -> dict:
    key = jax.random.key(seed)
    keys = iter(jax.random.split(key, 48))

    def nrm(shape, scale):
        return jax.random.normal(next(keys), shape, jnp.float32) * scale

    ml_centre = (jnp.arange(ML_CONV) == ML_CONV // 2).astype(jnp.float32)[None, :, None]
    rw_centre = (jnp.arange(RW_CONV) == RW_CONV // 2).astype(jnp.float32)[None, :, None]
    i_bias = nrm((DEPTH, 2, 1, ML_HEADS), 0.1)
    f_bias = jnp.linspace(3.0, 6.0, ML_HEADS, dtype=jnp.float32) + nrm((DEPTH, 2, 1, ML_HEADS), 0.1)
    return {
        "x_prompt": nrm((BATCH, SEQ, D_MODEL), 1.0),
        "x_sample": nrm((DEC_BATCH, DEC_SEQ, D_MODEL), 1.0),
        "ln0_g": 1.0 + nrm((D_MODEL,), 0.02),
        "ln0_b": nrm((D_MODEL,), 0.02),
        "w_in": nrm((DEPTH, D_MODEL, D_IN), D_MODEL ** -0.5),
        "na_rpb": nrm((DEPTH, NA_HEADS, 2 * NA_WIN_ROWS - 1, 2 * NA_WIN_COLS - 1), 0.1),
        "ml_conv": ml_centre + nrm((DEPTH, ML_CONV, 2 * ML_W), 0.3),
        "ml_gate_b": jnp.concatenate([i_bias, f_bias], axis=2).reshape(DEPTH, 4 * ML_HEADS),
        "ml_norm_g": 1.0 + nrm((DEPTH, ML_W), 0.02),
        "rw_conv": rw_centre + nrm((DEPTH, RW_CONV, 3 * RW_W), 0.3),
        "rw_w0": nrm((DEPTH, 2, RW_W), 0.5),
        "rw_w_up": nrm((DEPTH, 2, RW_DECAY_RANK, RW_W), 0.5 * RW_DECAY_RANK ** -0.5),
        "rw_a0": nrm((DEPTH, 2, RW_W), 0.1),
        "rw_a_up": nrm((DEPTH, 2, RW_AAA_RANK, RW_W), 0.5 * RW_AAA_RANK ** -0.5),
        "rw_g_up": nrm((DEPTH, RW_GATE_RANK, RW_W), RW_GATE_RANK ** -0.5),
        "rw_k_k": 0.85 + nrm((DEPTH, RW_W), 0.02),
        "rw_k_a": 1.0 + nrm((DEPTH, RW_W), 0.02),
        "rw_r_k": nrm((DEPTH, RW_HEADS, HEAD_DIM), 0.1),
        "rw_norm_g": 1.0 + nrm((DEPTH, RW_W), 0.02),
        "rw_norm_b": nrm((DEPTH, RW_W), 0.02),
        "w_br_a": nrm((DEPTH, NA_W, D_MODEL), NA_W ** -0.5),
        "w_br_b": nrm((DEPTH, ML_W, D_MODEL), ML_W ** -0.5),
        "w_br_c": nrm((DEPTH, RW_W, D_MODEL), RW_W ** -0.5),
        "w_out": nrm((DEPTH, D_MODEL, D_MODEL), BETA * D_MODEL ** -0.5),
        "ln1_g": 1.0 + nrm((DEPTH, D_MODEL), 0.02),
        "ln1_b": nrm((DEPTH, D_MODEL), 0.02),
        "w_router": nrm((D_MODEL, N_EXPERTS), D_MODEL ** -0.5),
        "router_bias": nrm((N_EXPERTS,), 0.01),
        "moe_w1": nrm((DEPTH, N_EXPERTS, D_MODEL, D_EXPERT), D_MODEL ** -0.5),
        "moe_w3": nrm((DEPTH, N_EXPERTS, D_MODEL, D_EXPERT), D_MODEL ** -0.5),
        "moe_w2": nrm((DEPTH, N_EXPERTS, D_EXPERT, D_MODEL), BETA * D_EXPERT ** -0.5),
        "ln2_g": 1.0 + nrm((DEPTH, D_MODEL), 0.02),
        "ln2_b": nrm((DEPTH, D_MODEL), 0.02),
    }


def reference(x_prompt, x_sample, ln0_g, ln0_b, w_in, na_rpb, ml_conv, ml_gate_b, ml_norm_g, rw_conv, rw_w0, rw_w_up,
              rw_a0, rw_a_up, rw_g_up, rw_k_k, rw_k_a, rw_r_k, rw_norm_g, rw_norm_b, w_br_a, w_br_b, w_br_c, w_out,
              ln1_g, ln1_b, w_router, router_bias, moe_w1, moe_w3, moe_w2, ln2_g, ln2_b):
    p = {
        "ln0_g": ln0_g, "ln0_b": ln0_b, "w_in": w_in, "na_rpb": na_rpb, "ml_conv": ml_conv,
        "ml_gate_b": ml_gate_b, "ml_norm_g": ml_norm_g, "rw_conv": rw_conv, "rw_w0": rw_w0,
        "rw_w_up": rw_w_up, "rw_a0": rw_a0, "rw_a_up": rw_a_up, "rw_g_up": rw_g_up, "rw_k_k": rw_k_k,
        "rw_k_a": rw_k_a, "rw_r_k": rw_r_k, "rw_norm_g": rw_norm_g, "rw_norm_b": rw_norm_b,
        "w_br_a": w_br_a, "w_br_b": w_br_b, "w_br_c": w_br_c, "w_out": w_out, "ln1_g": ln1_g,
        "ln1_b": ln1_b, "w_router": w_router, "router_bias": router_bias, "moe_w1": moe_w1,
        "moe_w3": moe_w3, "moe_w2": moe_w2, "ln2_g": ln2_g, "ln2_b": ln2_b,
    }
    y_prompt = _trunk(x_prompt, p)
    y_sample = _trunk(x_sample, p)
    return (y_prompt, y_sample)
```

```python
import functools

import jax
import jax.numpy as jnp
import numpy as np
from jax import lax
from jax.experimental import pallas as pl
from jax.experimental.pallas import tpu as pltpu

F32 = jnp.float32
BF16 = jnp.bfloat16

D_MODEL = 1024
DEPTH = 4
GRID_W = 64
HEAD_DIM = 64
N_HEADS = 4
BRANCH_W = N_HEADS * HEAD_DIM
CHUNK = 64
NA_WIN_ROWS = 8
NA_WIN_COLS = 16
NA_ROW_CLASSES = 8
ML_NORM_EPS = 1e-6
RW_RANK = 32
RW_DECAY_SCALE = 0.606531
RW_NORM_EPS = 64e-5
N_EXPERTS = 16
EXPERTS_PER_GROUP = 4
D_EXPERT = 512
ALPHA = (2 * DEPTH) ** 0.25
LN_EPS = 1e-5
NEG = -1e30

W_NA = 3 * BRANCH_W
W_MLQKV = 3 * BRANCH_W
W_MLO = BRANCH_W
W_RKV = 3 * BRANCH_W
W_SMALL = 256
W_MERGE = 3 * D_MODEL
PIECES = (W_NA, W_MLQKV, W_MLO, W_RKV, W_SMALL, W_MERGE)
D_INP = sum(PIECES)
SMALL_GATE_OFF = 192

VMEM_LIMIT = 56 * 1024 * 1024
TOKEN_TILE = 512


def _params(sem):
    return pltpu.CompilerParams(dimension_semantics=sem, vmem_limit_bytes=VMEM_LIMIT)


def _resident(shape):
    nd = len(shape)
    return pl.BlockSpec(shape, lambda *_: (0,) * nd, pipeline_mode=pl.Buffered(1))


def _ln(z, g, b):
    mu = jnp.mean(z, axis=-1, keepdims=True)
    zc = z - mu
    var = jnp.mean(zc * zc, axis=-1, keepdims=True)
    return zc * lax.rsqrt(var + LN_EPS) * g + b


def _ln_rows_body(x_ref, g_ref, b_ref, o_ref):
    o_ref[...] = _ln(x_ref[...], g_ref[...], b_ref[...])


def _ln_rows(x, g, b):
    n, d = x.shape
    tm = TOKEN_TILE
    return pl.pallas_call(
        _ln_rows_body,
        out_shape=jax.ShapeDtypeStruct((n, d), F32),
        grid=(n // tm,),
        in_specs=[pl.BlockSpec((tm, d), lambda i: (i, 0)), _resident((1, d)), _resident((1, d))],
        out_specs=pl.BlockSpec((tm, d), lambda i: (i, 0)),
        compiler_params=_params(("parallel",)),
        name="ln_rows",
    )(x, g.reshape(1, d), b.reshape(1, d))


def _in_proj_body(x_ref, w_ref, *o_refs):
    xb = x_ref[...].astype(BF16)
    off = 0
    for o_ref, width in zip(o_refs, PIECES):
        for c0 in range(0, width, 768):
            c1 = min(c0 + 768, width)
            o_ref[:, c0:c1] = jnp.dot(xb, w_ref[:, off + c0:off + c1], preferred_element_type=F32).astype(BF16)
        off += width


def _in_proj(x, w):
    n, d = x.shape
    tm = TOKEN_TILE
    return pl.pallas_call(
        _in_proj_body,
        out_shape=[jax.ShapeDtypeStruct((n, wd), BF16) for wd in PIECES],
        grid=(n // tm,),
        in_specs=[pl.BlockSpec((tm, d), lambda i: (i, 0)), _resident((d, D_INP))],
        out_specs=[pl.BlockSpec((tm, wd), lambda i: (i, 0)) for wd in PIECES],
        compiler_params=_params(("parallel",)),
        name="in_proj",
    )(x, w)


def _permute_w_in(w_in_l):
    na, mlqkv, mlo, mlg, rkv, wl, al, gl, merge = jnp.split(
        w_in_l, np.cumsum([768, 768, 256, 16, 768, 64, 64, 64])[:8].tolist(), axis=1)
    pad = jnp.zeros((w_in_l.shape[0], W_SMALL - 208), w_in_l.dtype)
    return jnp.concatenate([na, mlqkv, mlo, rkv, wl, al, gl, mlg, pad, merge], axis=1).astype(BF16)


def _mix_ln_body(x_ref, oa_ref, ob_ref, oc_ref, mg_ref, wa_ref, wb_ref, wc_ref, wo_ref, g_ref, b_ref, o_ref):
    mixed = None
    for j, (o_br, w_br) in enumerate(((oa_ref, wa_ref), (ob_ref, wb_ref), (oc_ref, wc_ref))):
        gate = jax.nn.sigmoid(mg_ref[:, j * D_MODEL:(j + 1) * D_MODEL].astype(F32))
        term = gate * jnp.dot(o_br[...], w_br[...], preferred_element_type=F32)
        mixed = term if mixed is None else mixed + term
    z = ALPHA * x_ref[...] + jnp.dot(mixed.astype(BF16), wo_ref[...], preferred_element_type=F32)
    o_ref[...] = _ln(z, g_ref[...], b_ref[...])


def _mix_ln(x, oa, ob, oc, merge, wa, wb, wc, wo, g, b):
    n, d = x.shape
    tm = TOKEN_TILE
    row = lambda wd: pl.BlockSpec((tm, wd), lambda i: (i, 0))
    return pl.pallas_call(
        _mix_ln_body,
        out_shape=jax.ShapeDtypeStruct((n, d), F32),
        grid=(n // tm,),
        in_specs=[row(d), row(BRANCH_W), row(BRANCH_W), row(BRANCH_W), row(W_MERGE),
                  _resident((BRANCH_W, d)), _resident((BRANCH_W, d)), _resident((BRANCH_W, d)), _resident((d, d)),
                  _resident((1, d)), _resident((1, d))],
        out_specs=row(d),
        compiler_params=_params(("parallel",)),
        name="mix_ln",
    )(x, oa, ob, oc, merge, wa.astype(BF16), wb.astype(BF16), wc.astype(BF16), wo.astype(BF16),
      g.reshape(1, d), b.reshape(1, d))


MOE_TILE = 1024


def _top2_sum(a, b, c, d):
    hi1, lo1 = jnp.maximum(a, b), jnp.minimum(a, b)
    hi2, lo2 = jnp.maximum(c, d), jnp.minimum(c, d)
    return jnp.maximum(hi1, hi2) + jnp.maximum(jnp.minimum(hi1, hi2), jnp.maximum(lo1, lo2))


def _router_gates(x, wrt, rbias):
    logits = lax.dot_general(wrt, x, (((1,), (1,)), ((), ())), precision=lax.Precision.HIGHEST,
                             preferred_element_type=F32)
    ex = jnp.exp(logits - jnp.max(logits, axis=0, keepdims=True))
    probs = ex / jnp.sum(ex, axis=0, keepdims=True)
    sel = probs + rbias
    rows = [sel[e:e + 1, :] for e in range(N_EXPERTS)]
    n_groups = N_EXPERTS // EXPERTS_PER_GROUP
    scores = [_top2_sum(*rows[EXPERTS_PER_GROUP * g:EXPERTS_PER_GROUP * (g + 1)]) for g in range(n_groups)]
    best, gidx = scores[0], jnp.zeros_like(scores[0], dtype=jnp.int32)
    for g in range(1, n_groups):
        better = scores[g] > best
        gidx = jnp.where(better, g, gidx)
        best = jnp.where(better, scores[g], best)
    picked = []
    for e in range(N_EXPERTS):
        g = e // EXPERTS_PER_GROUP
        rank = jnp.zeros_like(gidx)
        for e2 in range(EXPERTS_PER_GROUP * g, EXPERTS_PER_GROUP * (g + 1)):
            if e2 == e:
                continue
            ahead = (rows[e2] >= rows[e]) if e2 < e else (rows[e2] > rows[e])
            rank = rank + ahead.astype(jnp.int32)
        chosen = (gidx == g) & (rank < 2)
        picked.append(jnp.where(chosen, probs[e:e + 1, :], 0.0))
    total = picked[0]
    for p in picked[1:]:
        total = total + p
    return jnp.concatenate(picked, axis=0) / total


def _moe_ln_body(x_ref, wrt_ref, rb_ref, w1_ref, w3_ref, w2_ref, g_ref, b_ref, o_ref, xb_ref, gate_ref, acc_ref):
    e = pl.program_id(1)
    tm = x_ref.shape[0]

    @pl.when(e == 0)
    def _():
        x = x_ref[...]
        xb_ref[...] = x.astype(BF16)
        gates = _router_gates(x, wrt_ref[...], rb_ref[...])
        padded = jnp.concatenate([gates, jnp.zeros((128 - N_EXPERTS, tm), F32)], axis=0)
        gate_ref[...] = padded.T
        acc_ref[...] = jnp.zeros_like(acc_ref)

    xb = xb_ref[...]
    h1 = jnp.dot(xb, w1_ref[0], preferred_element_type=F32)
    h3 = jnp.dot(xb, w3_ref[0], preferred_element_type=F32)
    h = (h1 * jax.nn.sigmoid(h1) * h3).astype(BF16)
    y = jnp.dot(h, w2_ref[0], preferred_element_type=F32)
    lane = lax.broadcasted_iota(jnp.int32, (tm, 128), 1)
    gate_e = jnp.sum(jnp.where(lane == e, gate_ref[...], 0.0), axis=1, keepdims=True)
    acc_ref[...] += gate_e * y

    @pl.when(e == N_EXPERTS - 1)
    def _():
        o_ref[...] = _ln(ALPHA * x_ref[...] + acc_ref[...], g_ref[...], b_ref[...])


def _moe_ln(x, w_router, router_bias, w1, w3, w2, g, b):
    n, d = x.shape
    tm = MOE_TILE
    return pl.pallas_call(
        _moe_ln_body,
        out_shape=jax.ShapeDtypeStruct((n, d), F32),
        grid=(n // tm, N_EXPERTS),
        in_specs=[pl.BlockSpec((tm, d), lambda i, e: (i, 0)),
                  _resident((N_EXPERTS, d)), _resident((N_EXPERTS, 1)),
                  pl.BlockSpec((1, d, D_EXPERT), lambda i, e: (e, 0, 0)),
                  pl.BlockSpec((1, d, D_EXPERT), lambda i, e: (e, 0, 0)),
                  pl.BlockSpec((1, D_EXPERT, d), lambda i, e: (e, 0, 0)),
                  _resident((1, d)), _resident((1, d))],
        out_specs=pl.BlockSpec((tm, d), lambda i, e: (i, 0)),
        scratch_shapes=[pltpu.VMEM((tm, d), BF16), pltpu.VMEM((tm, 128), F32), pltpu.VMEM((tm, d), F32)],
        compiler_params=_params(("parallel", "arbitrary")),
        name="moe_ln",
    )(x, w_router.T, router_bias.reshape(N_EXPERTS, 1), w1.astype(BF16), w3.astype(BF16), w2.astype(BF16),
      g.reshape(1, d), b.reshape(1, d))


def _na_bias_table(rpb, rows):
    assert rows >= NA_WIN_ROWS
    cols = np.arange(GRID_W)
    win = np.clip(cols - NA_WIN_COLS // 2, 0, GRID_W - NA_WIN_COLS)
    rel = cols[None, :] - cols[:, None]
    ok = (cols[None, :] >= win[:, None]) & (cols[None, :] < win[:, None] + NA_WIN_COLS)
    col_idx = np.clip(rel + NA_WIN_COLS - 1, 0, 2 * NA_WIN_COLS - 2)
    row_idx = np.arange(NA_WIN_ROWS)[None, :] + (NA_WIN_ROWS - 1) - np.arange(NA_ROW_CLASSES)[:, None]
    t = rpb.astype(F32)[:, row_idx[:, :, None, None], col_idx[None, None, :, :]]
    t = jnp.where(ok[None, None, None], t, NEG)
    return jnp.transpose(t, (0, 1, 3, 2, 4)).reshape(rpb.shape[0], NA_ROW_CLASSES, GRID_W, NA_WIN_ROWS * GRID_W)


def _na_body(qkv_ref, bias_ref, o_ref):
    rows = qkv_ref.shape[1] // GRID_W
    band = NA_WIN_ROWS * GRID_W
    for h in range(N_HEADS):
        qc = slice(h * HEAD_DIM, (h + 1) * HEAD_DIM)
        kc = slice(BRANCH_W + h * HEAD_DIM, BRANCH_W + (h + 1) * HEAD_DIM)
        vc = slice(2 * BRANCH_W + h * HEAD_DIM, 2 * BRANCH_W + (h + 1) * HEAD_DIM)

        def row_step(r, carry):
            rs = jnp.clip(r - NA_WIN_ROWS // 2, 0, rows - NA_WIN_ROWS)
            cls = jnp.where(r < NA_WIN_ROWS // 2, r, NA_WIN_ROWS // 2 + jnp.maximum(r - (rows - NA_WIN_ROWS // 2), 0))
            q0 = pl.multiple_of(r * GRID_W, GRID_W)
            k0 = pl.multiple_of(rs * GRID_W, GRID_W)
            q = qkv_ref[0, pl.ds(q0, GRID_W), qc]
            k = qkv_ref[0, pl.ds(k0, band), kc]
            v = qkv_ref[0, pl.ds(k0, band), vc]
            s = lax.dot_general(q, k, (((1,), (1,)), ((), ())), preferred_element_type=F32)
            s = s * (HEAD_DIM ** -0.5) + bias_ref[h, cls]
            p = jnp.exp(s - jnp.max(s, axis=1, keepdims=True))
            o = jnp.dot(p.astype(BF16), v, preferred_element_type=F32) / jnp.sum(p, axis=1, keepdims=True)
            o_ref[0, pl.ds(q0, GRID_W), qc] = o.astype(BF16)
            return carry

        lax.fori_loop(0, rows, row_step, 0)


def _na_attention(qkv, bias):
    bsz, s, _ = qkv.shape
    return pl.pallas_call(
        _na_body,
        out_shape=jax.ShapeDtypeStruct((bsz, s, BRANCH_W), BF16),
        grid=(bsz,),
        in_specs=[pl.BlockSpec((1, s, W_NA), lambda b: (b, 0, 0)), _resident(bias.shape)],
        out_specs=pl.BlockSpec((1, s, BRANCH_W), lambda b: (b, 0, 0)),
        compiler_params=_params(("parallel",)),
        name="na_attention",
    )(qkv, bias)


def _eye(n):
    return lax.broadcasted_iota(jnp.int32, (n, n), 0) == lax.broadcasted_iota(jnp.int32, (n, n), 1)


def _row_to_col(row, eye):
    return jnp.sum(jnp.where(eye, row, 0.0), axis=1, keepdims=True)


def _conv3_rows(ref, c, nch, cols, w_ref):
    seq = nch * CHUNK
    r0 = pl.multiple_of(c * CHUNK, CHUNK)
    x = ref[0, pl.ds(r0, CHUNK), cols].astype(F32)
    lo = pl.multiple_of(jnp.maximum(r0 - 16, 0), 16)
    hi = pl.multiple_of(jnp.minimum(r0 + CHUNK, seq - 16), 16)
    prev = ref[0, pl.ds(lo, 16), cols][15:16, :].astype(F32) * (c > 0).astype(F32)
    nxt = ref[0, pl.ds(hi, 16), cols][0:1, :].astype(F32) * (c < nch - 1).astype(F32)
    rid = lax.broadcasted_iota(jnp.int32, x.shape, 0)
    xm = jnp.where(rid == 0, prev, pltpu.roll(x, 1, axis=0))
    xp = jnp.where(rid == CHUNK - 1, nxt, pltpu.roll(x, CHUNK - 1, axis=0))
    return w_ref[0:1, :] * xm + w_ref[1:2, :] * x + w_ref[2:3, :] * xp


def _head_norm(h, eps):
    mu = jnp.mean(h, axis=-1, keepdims=True)
    hc = h - mu
    return hc * lax.rsqrt(jnp.mean(hc * hc, axis=-1, keepdims=True) + eps)


def _dot_nt(a, b, **kw):
    return lax.dot_general(a, b, (((1,), (1,)), ((), ())), preferred_element_type=F32, **kw)


def _dot_tn(a, b, **kw):
    return lax.dot_general(a, b, (((0,), (0,)), ((), ())), preferred_element_type=F32, **kw)


def _mlstm_body(qkv_ref, o_ref_in, g_ref, gb_ref, cw_ref, ng_ref, out_ref,
                qk_s, hf_s, ig_s, b_s, ct_s, n_s, mc_s, bl_s, cprev_s, nprev_s, mprev_s):
    seq = qkv_ref.shape[1]
    nch = seq // CHUNK
    eye = _eye(CHUNK)
    ri = lax.broadcasted_iota(jnp.int32, (CHUNK, CHUNK), 0)
    ci = lax.broadcasted_iota(jnp.int32, (CHUNK, CHUNK), 1)

    def conv_step(c, carry):
        y = _conv3_rows(qkv_ref, c, nch, slice(0, 2 * BRANCH_W), cw_ref)
        y = y * jax.nn.sigmoid(y)
        scale = jnp.where(lax.broadcasted_iota(jnp.int32, (1, 2 * BRANCH_W), 1) < BRANCH_W, 1.0, HEAD_DIM ** -0.5)
        qk_s[pl.ds(pl.multiple_of(c * CHUNK, CHUNK), CHUNK), :] = (y * scale).astype(BF16)
        return carry

    lax.fori_loop(0, nch, conv_step, 0)

    for h in range(N_HEADS):
        hc = slice(h * HEAD_DIM, (h + 1) * HEAD_DIM)
        kcol = slice(BRANCH_W + h * HEAD_DIM, BRANCH_W + (h + 1) * HEAD_DIM)
        vcol = slice(2 * BRANCH_W + h * HEAD_DIM, 2 * BRANCH_W + (h + 1) * HEAD_DIM)
        for rev in (False, True):
            gi = (8 if rev else 0) + h
            ig_s[...] = g_ref[0, gi] + gb_ref[gi]
            fpre = g_ref[0, gi + N_HEADS] + gb_ref[gi + N_HEADS]
            lf = jnp.minimum(fpre, 0.0) - jnp.log1p(jnp.exp(-jnp.abs(fpre)))
            cum = (ri >= ci) if rev else (ri <= ci)
            b_s[...] = jnp.dot(lf, cum.astype(F32), precision=lax.Precision.HIGHEST, preferred_element_type=F32)
            tri = (ci >= ri) if rev else (ci <= ri)
            last = slice(0, 1) if rev else slice(CHUNK - 1, CHUNK)

            def chunk_stats(c, carry):
                rows = pl.ds(pl.multiple_of(c * CHUNK, CHUNK), CHUNK)
                b_r = b_s[pl.ds(c, 1), :]
                bl = b_r[:, last]
                a_r = bl - b_r + ig_s[pl.ds(c, 1), :]
                mc = jnp.max(a_r, axis=1, keepdims=True)
                wa_c = _row_to_col(jnp.exp(a_r - mc), eye)
                kw = qk_s[rows, kcol].astype(F32) * wa_c
                ct_s[c] = _dot_tn(kw.astype(BF16), qkv_ref[0, rows, vcol])
                n_s[c] = jnp.sum(kw, axis=0, keepdims=True)
                mc_s[c] = jnp.broadcast_to(mc, (1, 128))
                bl_s[c] = jnp.broadcast_to(bl, (1, 128))
                return carry

            lax.fori_loop(0, nch, chunk_stats, 0)

            def scan_step(i, carry):
                ct, n, m = carry
                c = (nch - 1 - i) if rev else i
                cprev_s[c] = ct
                nprev_s[c] = n
                mprev_s[c] = m
                bl, mc = bl_s[c], mc_s[c]
                m_new = jnp.maximum(bl + m, mc)
                sp = jnp.exp(bl + m - m_new)[:, :HEAD_DIM]
                sc = jnp.exp(mc - m_new)[:, :HEAD_DIM]
                return sp * ct + sc * ct_s[c], sp * n + sc * n_s[c], m_new

            lax.fori_loop(0, nch, scan_step, (jnp.zeros((HEAD_DIM, HEAD_DIM), F32), jnp.zeros((1, HEAD_DIM), F32),
                                              jnp.full((1, 128), NEG, F32)))

            def chunk_out(c, carry):
                rows = pl.ds(pl.multiple_of(c * CHUNK, CHUNK), CHUNK)
                b_r = b_s[pl.ds(c, 1), :]
                b_c = _row_to_col(b_r, eye)
                dmat = jnp.where(tri, b_c - b_r + ig_s[pl.ds(c, 1), :], NEG)
                m_intra = jnp.max(dmat, axis=1, keepdims=True)
                g_c = b_c + mprev_s[c][:, 0:1]
                m_t = jnp.maximum(g_c, m_intra)
                inter = jnp.exp(g_c - m_t)
                q = qk_s[rows, hc]
                k = qk_s[rows, kcol]
                v = qkv_ref[0, rows, vcol]
                w_intra = _dot_nt(q, k) * jnp.exp(dmat - m_t)
                num = inter * jnp.dot(q, cprev_s[c].astype(BF16), preferred_element_type=F32) \
                    + jnp.dot(w_intra.astype(BF16), v, preferred_element_type=F32)
                den = inter * jnp.sum(q.astype(F32) * nprev_s[c], axis=1, keepdims=True) \
                    + jnp.sum(w_intra, axis=1, keepdims=True)
                hh = num / jnp.maximum(jnp.abs(den), jnp.exp(-m_t))
                if not rev:
                    hf_s[rows, hc] = hh
                else:
                    tot = _head_norm(hh + hf_s[rows, hc], ML_NORM_EPS) * ng_ref[:, hc]
                    out_ref[0, rows, hc] = (tot * jax.nn.sigmoid(o_ref_in[0, rows, hc].astype(F32))).astype(BF16)
                return carry

            lax.fori_loop(0, nch, chunk_out, 0)


def _mlstm(qkv, o_pre, small, conv_w, gate_b, norm_g):
    bsz, s, _ = qkv.shape
    nch = s // CHUNK
    n_gate = 4 * N_HEADS
    gates = small[..., SMALL_GATE_OFF:SMALL_GATE_OFF + n_gate].astype(F32)
    gates = jnp.transpose(gates, (0, 2, 1)).reshape(bsz, n_gate, nch, CHUNK)
    gate_b = jnp.broadcast_to(gate_b.astype(F32).reshape(n_gate, 1, 1), (n_gate, 1, CHUNK))
    row = lambda shape: pltpu.VMEM(shape, F32)
    return pl.pallas_call(
        _mlstm_body,
        out_shape=jax.ShapeDtypeStruct((bsz, s, BRANCH_W), BF16),
        grid=(bsz,),
        in_specs=[pl.BlockSpec((1, s, W_MLQKV), lambda b: (b, 0, 0)),
                  pl.BlockSpec((1, s, BRANCH_W), lambda b: (b, 0, 0)),
                  pl.BlockSpec((1, n_gate, nch, CHUNK), lambda b: (b, 0, 0, 0)),
                  _resident((n_gate, 1, CHUNK)), _resident((3, 2 * BRANCH_W)), _resident((1, BRANCH_W))],
        out_specs=pl.BlockSpec((1, s, BRANCH_W), lambda b: (b, 0, 0)),
        scratch_shapes=[pltpu.VMEM((s, 2 * BRANCH_W), BF16), row((s, BRANCH_W)), row((nch, CHUNK)), row((nch, CHUNK)),
                        row((nch, HEAD_DIM, HEAD_DIM)), row((nch, 1, HEAD_DIM)), row((nch, 1, 128)), row((nch, 1, 128)),
                        row((nch, HEAD_DIM, HEAD_DIM)), row((nch, 1, HEAD_DIM)), row((nch, 1, 128))],
        compiler_params=_params(("parallel",)),
        name="mlstm",
    )(qkv, o_pre, gates, gate_b, conv_w.astype(F32), norm_g.astype(F32).reshape(1, BRANCH_W))


_HI = dict(precision=lax.Precision.HIGHEST)


def _rwkv_body(rkv_ref, sm_ref, cw_ref, w0_ref, wup_ref, a0_ref, aup_ref, gup_ref, kk_ref, ka_ref, rk_ref, ng_ref, nb_ref,
               out_ref, y_s, st_s):
    seq = rkv_ref.shape[1]
    nch = seq // CHUNK
    ri = lax.broadcasted_iota(jnp.int32, (CHUNK, CHUNK), 0)
    ci = lax.broadcasted_iota(jnp.int32, (CHUNK, CHUNK), 1)
    ident = (ri == ci).astype(F32)

    def low_rank(x, d, up_ref, bias_ref):
        return bias_ref[d:d + 1, :] + jnp.dot(x[:, d * RW_RANK:(d + 1) * RW_RANK], up_ref[d], preferred_element_type=F32, **_HI)

    for rev in (False, True):
        d = 1 if rev else 0
        strict = (ci > ri) if rev else (ci < ri)
        incl = (ci >= ri) if rev else (ci <= ri)
        cum = incl.astype(F32)
        last = slice(0, 1) if rev else slice(CHUNK - 1, CHUNK)
        st_s[...] = jnp.zeros_like(st_s)

        def chunk_step(i, carry):
            c = (nch - 1 - i) if rev else i
            rows = pl.ds(pl.multiple_of(c * CHUNK, CHUNK), CHUNK)
            rkv = _conv3_rows(rkv_ref, c, nch, slice(0, 3 * BRANCH_W), cw_ref)
            r, k, v = rkv[:, :BRANCH_W], rkv[:, BRANCH_W:2 * BRANCH_W], rkv[:, 2 * BRANCH_W:]
            sm = sm_ref[0, rows, :].astype(F32)
            w_lo, a_lo, g_lo = jnp.tanh(sm[:, :2 * RW_RANK]), sm[:, 2 * RW_RANK:4 * RW_RANK], sm[:, 4 * RW_RANK:6 * RW_RANK]
            lw = -RW_DECAY_SCALE * jax.nn.sigmoid(low_rank(w_lo, d, wup_ref, w0_ref))
            a = jax.nn.sigmoid(low_rank(a_lo, d, aup_ref, a0_ref))
            kd = k * (1.0 + (a - 1.0) * ka_ref[...])
            kk_raw = k * kk_ref[...]
            lp_inc = jnp.dot(cum, lw, preferred_element_type=F32, **_HI)
            p_end = jnp.exp(lp_inc[last, :])
            e_inc = jnp.exp(lp_inc)
            e_inv = jnp.exp(-lp_inc)
            e_exc = jnp.exp(lp_inc - lw)
            if rev:
                a_f = jax.nn.sigmoid(low_rank(a_lo, 0, aup_ref, a0_ref))
                kd_sum = kd + k * (1.0 + (a_f - 1.0) * ka_ref[...])
                gate = jnp.dot(jax.nn.sigmoid(g_lo), gup_ref[...], preferred_element_type=F32, **_HI)
            for h in range(N_HEADS):
                hc = slice(h * HEAD_DIM, (h + 1) * HEAD_DIM)
                kk_h = kk_raw[:, hc]
                kk_h = kk_h / jnp.maximum(jnp.sqrt(jnp.sum(kk_h * kk_h, axis=1, keepdims=True)), 1e-12)
                b_hat = kk_h * a[:, hc] * e_inv[:, hc]
                k_hat = kd[:, hc] * e_inv[:, hc]
                kk_t = kk_h * e_exc[:, hc]
                r_t = r[:, hc] * e_inc[:, hc]
                v_h = v[:, hc]
                p4 = _dot_nt(jnp.concatenate([kk_t, r_t], axis=0), jnp.concatenate([b_hat, k_hat], axis=0), **_HI)
                nil = jnp.where(strict, -p4[:CHUNK, :CHUNK], 0.0)
                inv = ident + nil
                for _ in range(5):
                    nil = jnp.dot(nil, nil, preferred_element_type=F32, **_HI)
                    inv = inv + jnp.dot(inv, nil, preferred_element_type=F32, **_HI)
                akv = jnp.dot(jnp.where(strict, p4[:CHUNK, CHUNK:], 0.0), v_h, preferred_element_type=F32, **_HI)
                m12 = jnp.dot(inv, jnp.concatenate([kk_t, akv], axis=1), preferred_element_type=F32, **_HI)
                m1, m2 = m12[:, :HEAD_DIM], m12[:, HEAD_DIM:]
                g_mat = (ident - _dot_tn(m1, b_hat, **_HI)) * p_end[:, hc]
                h_mat = _dot_tn(jnp.concatenate([v_h, m2], axis=0), jnp.concatenate([k_hat, -b_hat], axis=0), **_HI) * p_end[:, hc]
                qy = jnp.dot(jnp.where(incl, p4[CHUNK:, :CHUNK], 0.0), m12, preferred_element_type=F32, **_HI)
                q_mat = r_t - qy[:, :HEAD_DIM]
                y0 = jnp.dot(jnp.where(incl, p4[CHUNK:, CHUNK:], 0.0), v_h, preferred_element_type=F32, **_HI) - qy[:, HEAD_DIM:]
                s0 = st_s[h]
                y = _dot_nt(q_mat, s0, **_HI) + y0
                st_s[h] = jnp.dot(s0, g_mat, preferred_element_type=F32, **_HI) + h_mat
                if not rev:
                    y_s[rows, hc] = y
                else:
                    tot = _head_norm(y + y_s[rows, hc], RW_NORM_EPS) * ng_ref[:, hc] + nb_ref[:, hc]
                    bonus = jnp.sum(r[:, hc] * kd_sum[:, hc] * rk_ref[:, hc], axis=1, keepdims=True) * v_h
                    out_ref[0, rows, hc] = ((tot + bonus) * gate[:, hc]).astype(BF16)
            return carry

        lax.fori_loop(0, nch, chunk_step, 0)


def _rwkv(rkv, small, conv_w, w0, w_up, a0, a_up, g_up, k_k, k_a, r_k, norm_g, norm_b):
    bsz, s, _ = rkv.shape
    vec = lambda t: t.astype(F32).reshape(1, BRANCH_W)
    f = lambda t: t.astype(F32)
    return pl.pallas_call(
        _rwkv_body,
        out_shape=jax.ShapeDtypeStruct((bsz, s, BRANCH_W), BF16),
        grid=(bsz,),
        in_specs=[pl.BlockSpec((1, s, W_RKV), lambda b: (b, 0, 0)),
                  pl.BlockSpec((1, s, W_SMALL), lambda b: (b, 0, 0)),
                  _resident((3, 3 * BRANCH_W)), _resident((2, BRANCH_W)), _resident((2, RW_RANK, BRANCH_W)),
                  _resident((2, BRANCH_W)), _resident((2, RW_RANK, BRANCH_W)), _resident((2 * RW_RANK, BRANCH_W)),
                  _resident((1, BRANCH_W)), _resident((1, BRANCH_W)), _resident((1, BRANCH_W)),
                  _resident((1, BRANCH_W)), _resident((1, BRANCH_W))],
        out_specs=pl.BlockSpec((1, s, BRANCH_W), lambda b: (b, 0, 0)),
        scratch_shapes=[pltpu.VMEM((s, BRANCH_W), F32), pltpu.VMEM((N_HEADS, HEAD_DIM, HEAD_DIM), F32)],
        compiler_params=_params(("parallel",)),
        name="rwkv7",
    )(rkv, small, f(conv_w), f(w0), f(w_up), f(a0), f(a_up), f(g_up), vec(k_k), vec(k_a), vec(r_k), vec(norm_g), vec(norm_b))


def _trunk(x, p, w_in_perm):
    bsz, s, d = x.shape
    n = bsz * s
    seq = lambda t: t.reshape(bsz, s, t.shape[-1])
    flat = lambda t: t.reshape(n, t.shape[-1])
    xf = _ln_rows(x.reshape(n, d), p["ln0_g"], p["ln0_b"])
    for l in range(DEPTH):
        na, mlqkv, mlo, rkv, small, merge = _in_proj(xf, w_in_perm[l])
        o_a = _na_attention(seq(na), _na_bias_table(p["na_rpb"][l], s // GRID_W))
        o_b = _mlstm(seq(mlqkv), seq(mlo), seq(small), p["ml_conv"][l], p["ml_gate_b"][l], p["ml_norm_g"][l])
        o_c = _rwkv(seq(rkv), seq(small), p["rw_conv"][l], p["rw_w0"][l], p["rw_w_up"][l], p["rw_a0"][l], p["rw_a_up"][l],
                    p["rw_g_up"][l], p["rw_k_k"][l], p["rw_k_a"][l], p["rw_r_k"][l], p["rw_norm_g"][l], p["rw_norm_b"][l])
        xf = _mix_ln(xf, flat(o_a), flat(o_b), flat(o_c), merge, p["w_br_a"][l], p["w_br_b"][l], p["w_br_c"][l],
                     p["w_out"][l], p["ln1_g"][l], p["ln1_b"][l])
        xf = _moe_ln(xf, p["w_router"], p["router_bias"], p["moe_w1"][l], p["moe_w3"][l], p["moe_w2"][l],
                     p["ln2_g"][l], p["ln2_b"][l])
    return xf.reshape(bsz, s, d)


def kernel(x_prompt, x_sample, ln0_g, ln0_b, w_in, na_rpb, ml_conv, ml_gate_b, ml_norm_g, rw_conv, rw_w0, rw_w_up,
           rw_a0, rw_a_up, rw_g_up, rw_k_k, rw_k_a, rw_r_k, rw_norm_g, rw_norm_b, w_br_a, w_br_b, w_br_c, w_out,
           ln1_g, ln1_b, w_router, router_bias, moe_w1, moe_w3, moe_w2, ln2_g, ln2_b):
    p = {
        "ln0_g": ln0_g, "ln0_b": ln0_b, "na_rpb": na_rpb, "ml_conv": ml_conv,
        "ml_gate_b": ml_gate_b, "ml_norm_g": ml_norm_g, "rw_conv": rw_conv, "rw_w0": rw_w0,
        "rw_w_up": rw_w_up, "rw_a0": rw_a0, "rw_a_up": rw_a_up, "rw_g_up": rw_g_up, "rw_k_k": rw_k_k,
        "rw_k_a": rw_k_a, "rw_r_k": rw_r_k, "rw_norm_g": rw_norm_g, "rw_norm_b": rw_norm_b,
        "w_br_a": w_br_a, "w_br_b": w_br_b, "w_br_c": w_br_c, "w_out": w_out, "ln1_g": ln1_g,
        "ln1_b": ln1_b, "w_router": w_router, "router_bias": router_bias, "moe_w1": moe_w1,
        "moe_w3": moe_w3, "moe_w2": moe_w2, "ln2_g": ln2_g, "ln2_b": ln2_b,
    }
    w_in_perm = [_permute_w_in(w_in[l]) for l in range(DEPTH)]
    return _trunk(x_prompt, p, w_in_perm), _trunk(x_sample, p, w_in_perm)
```

```python
import functools

import jax
import jax.numpy as jnp
import numpy as np
from jax import lax
from jax.experimental import pallas as pl
from jax.experimental.pallas import tpu as pltpu

F32 = jnp.float32
BF16 = jnp.bfloat16

D_MODEL = 1024
DEPTH = 4
GRID_W = 64
HEAD_DIM = 64
N_HEADS = 4
BRANCH_W = N_HEADS * HEAD_DIM
CHUNK = 64
NA_WIN_ROWS = 8
NA_WIN_COLS = 16
NA_ROW_CLASSES = 8
ML_NORM_EPS = 1e-6
RW_RANK = 32
RW_DECAY_SCALE = 0.606531
RW_NORM_EPS = 64e-5
N_EXPERTS = 16
EXPERTS_PER_GROUP = 4
D_EXPERT = 512
ALPHA = (2 * DEPTH) ** 0.25
LN_EPS = 1e-5
NEG = -1e30

W_NA = 3 * BRANCH_W
W_MLQKV = 3 * BRANCH_W
W_MLO = BRANCH_W
W_RKV = 3 * BRANCH_W
W_SMALL = 256
W_MERGE = 3 * D_MODEL
PIECES = (W_NA, W_MLQKV, W_MLO, W_RKV, W_SMALL, W_MERGE)
D_INP = sum(PIECES)
SMALL_GATE_OFF = 192

VMEM_LIMIT = 56 * 1024 * 1024
TOKEN_TILE = 512


def _params(sem):
    return pltpu.CompilerParams(dimension_semantics=sem, vmem_limit_bytes=VMEM_LIMIT)


def _resident(shape):
    nd = len(shape)
    return pl.BlockSpec(shape, lambda *_: (0,) * nd, pipeline_mode=pl.Buffered(1))


def _ln(z, g, b):
    mu = jnp.mean(z, axis=-1, keepdims=True)
    zc = z - mu
    var = jnp.mean(zc * zc, axis=-1, keepdims=True)
    return zc * lax.rsqrt(var + LN_EPS) * g + b


def _ln_rows_body(x_ref, g_ref, b_ref, o_ref):
    o_ref[...] = _ln(x_ref[...], g_ref[...], b_ref[...])


def _ln_rows(x, g, b):
    n, d = x.shape
    tm = TOKEN_TILE
    return pl.pallas_call(
        _ln_rows_body,
        out_shape=jax.ShapeDtypeStruct((n, d), F32),
        grid=(n // tm,),
        in_specs=[pl.BlockSpec((tm, d), lambda i: (i, 0)), _resident((1, d)), _resident((1, d))],
        out_specs=pl.BlockSpec((tm, d), lambda i: (i, 0)),
        compiler_params=_params(("parallel",)),
        name="ln_rows",
    )(x, g.reshape(1, d), b.reshape(1, d))


def _in_proj_body(x_ref, w_ref, *o_refs):
    xb = x_ref[...].astype(BF16)
    off = 0
    for o_ref, width in zip(o_refs, PIECES):
        for c0 in range(0, width, 768):
            c1 = min(c0 + 768, width)
            o_ref[:, c0:c1] = jnp.dot(xb, w_ref[:, off + c0:off + c1], preferred_element_type=F32).astype(BF16)
        off += width


def _in_proj(x, w):
    n, d = x.shape
    tm = TOKEN_TILE
    return pl.pallas_call(
        _in_proj_body,
        out_shape=[jax.ShapeDtypeStruct((n, wd), BF16) for wd in PIECES],
        grid=(n // tm,),
        in_specs=[pl.BlockSpec((tm, d), lambda i: (i, 0)), _resident((d, D_INP))],
        out_specs=[pl.BlockSpec((tm, wd), lambda i: (i, 0)) for wd in PIECES],
        compiler_params=_params(("parallel",)),
        name="in_proj",
    )(x, w)


def _permute_w_in(w_in_l):
    na, mlqkv, mlo, mlg, rkv, wl, al, gl, merge = jnp.split(
        w_in_l, np.cumsum([768, 768, 256, 16, 768, 64, 64, 64])[:8].tolist(), axis=1)
    pad = jnp.zeros((w_in_l.shape[0], W_SMALL - 208), w_in_l.dtype)
    return jnp.concatenate([na, mlqkv, mlo, rkv, wl, al, gl, mlg, pad, merge], axis=1).astype(BF16)


def _mix_ln_body(x_ref, oa_ref, ob_ref, oc_ref, mg_ref, wa_ref, wb_ref, wc_ref, wo_ref, g_ref, b_ref, o_ref):
    mixed = None
    for j, (o_br, w_br) in enumerate(((oa_ref, wa_ref), (ob_ref, wb_ref), (oc_ref, wc_ref))):
        gate = jax.nn.sigmoid(mg_ref[:, j * D_MODEL:(j + 1) * D_MODEL].astype(F32))
        term = gate * jnp.dot(o_br[...], w_br[...], preferred_element_type=F32)
        mixed = term if mixed is None else mixed + term
    z = ALPHA * x_ref[...] + jnp.dot(mixed.astype(BF16), wo_ref[...], preferred_element_type=F32)
    o_ref[...] = _ln(z, g_ref[...], b_ref[...])


def _mix_ln(x, oa, ob, oc, merge, wa, wb, wc, wo, g, b):
    n, d = x.shape
    tm = TOKEN_TILE
    row = lambda wd: pl.BlockSpec((tm, wd), lambda i: (i, 0))
    return pl.pallas_call(
        _mix_ln_body,
        out_shape=jax.ShapeDtypeStruct((n, d), F32),
        grid=(n // tm,),
        in_specs=[row(d), row(BRANCH_W), row(BRANCH_W), row(BRANCH_W), row(W_MERGE),
                  _resident((BRANCH_W, d)), _resident((BRANCH_W, d)), _resident((BRANCH_W, d)), _resident((d, d)),
                  _resident((1, d)), _resident((1, d))],
        out_specs=row(d),
        compiler_params=_params(("parallel",)),
        name="mix_ln",
    )(x, oa, ob, oc, merge, wa.astype(BF16), wb.astype(BF16), wc.astype(BF16), wo.astype(BF16),
      g.reshape(1, d), b.reshape(1, d))


MOE_TILE = 1024


def _top2_sum(a, b, c, d):
    hi1, lo1 = jnp.maximum(a, b), jnp.minimum(a, b)
    hi2, lo2 = jnp.maximum(c, d), jnp.minimum(c, d)
    return jnp.maximum(hi1, hi2) + jnp.maximum(jnp.minimum(hi1, hi2), jnp.maximum(lo1, lo2))


def _router_gates(x, wrt, rbias):
    logits = lax.dot_general(wrt, x, (((1,), (1,)), ((), ())), precision=lax.Precision.HIGHEST,
                             preferred_element_type=F32)
    ex = jnp.exp(logits - jnp.max(logits, axis=0, keepdims=True))
    probs = ex / jnp.sum(ex, axis=0, keepdims=True)
    sel = probs + rbias
    rows = [sel[e:e + 1, :] for e in range(N_EXPERTS)]
    n_groups = N_EXPERTS // EXPERTS_PER_GROUP
    scores = [_top2_sum(*rows[EXPERTS_PER_GROUP * g:EXPERTS_PER_GROUP * (g + 1)]) for g in range(n_groups)]
    best, gidx = scores[0], jnp.zeros_like(scores[0], dtype=jnp.int32)
    for g in range(1, n_groups):
        better = scores[g] > best
        gidx = jnp.where(better, g, gidx)
        best = jnp.where(better, scores[g], best)
    picked = []
    for e in range(N_EXPERTS):
        g = e // EXPERTS_PER_GROUP
        rank = jnp.zeros_like(gidx)
        for e2 in range(EXPERTS_PER_GROUP * g, EXPERTS_PER_GROUP * (g + 1)):
            if e2 == e:
                continue
            ahead = (rows[e2] >= rows[e]) if e2 < e else (rows[e2] > rows[e])
            rank = rank + ahead.astype(jnp.int32)
        chosen = (gidx == g) & (rank < 2)
        picked.append(jnp.where(chosen, probs[e:e + 1, :], 0.0))
    total = picked[0]
    for p in picked[1:]:
        total = total + p
    return jnp.concatenate(picked, axis=0) / total


def _moe_ln_body(x_ref, wrt_ref, rb_ref, w1_ref, w3_ref, w2_ref, g_ref, b_ref, o_ref, xb_ref, gate_ref, acc_ref):
    e = pl.program_id(1)
    tm = x_ref.shape[0]

    @pl.when(e == 0)
    def _():
        x = x_ref[...]
        xb_ref[...] = x.astype(BF16)
        gates = _router_gates(x, wrt_ref[...], rb_ref[...])
        padded = jnp.concatenate([gates, jnp.zeros((128 - N_EXPERTS, tm), F32)], axis=0)
        gate_ref[...] = padded.T
        acc_ref[...] = jnp.zeros_like(acc_ref)

    xb = xb_ref[...]
    h1 = jnp.dot(xb, w1_ref[0], preferred_element_type=F32)
    h3 = jnp.dot(xb, w3_ref[0], preferred_element_type=F32)
    h = (h1 * jax.nn.sigmoid(h1) * h3).astype(BF16)
    y = jnp.dot(h, w2_ref[0], preferred_element_type=F32)
    lane = lax.broadcasted_iota(jnp.int32, (tm, 128), 1)
    gate_e = jnp.sum(jnp.where(lane == e, gate_ref[...], 0.0), axis=1, keepdims=True)
    acc_ref[...] += gate_e * y

    @pl.when(e == N_EXPERTS - 1)
    def _():
        o_ref[...] = _ln(ALPHA * x_ref[...] + acc_ref[...], g_ref[...], b_ref[...])


def _moe_ln(x, w_router, router_bias, w1, w3, w2, g, b):
    n, d = x.shape
    tm = MOE_TILE
    return pl.pallas_call(
        _moe_ln_body,
        out_shape=jax.ShapeDtypeStruct((n, d), F32),
        grid=(n // tm, N_EXPERTS),
        in_specs=[pl.BlockSpec((tm, d), lambda i, e: (i, 0)),
                  _resident((N_EXPERTS, d)), _resident((N_EXPERTS, 1)),
                  pl.BlockSpec((1, d, D_EXPERT), lambda i, e: (e, 0, 0)),
                  pl.BlockSpec((1, d, D_EXPERT), lambda i, e: (e, 0, 0)),
                  pl.BlockSpec((1, D_EXPERT, d), lambda i, e: (e, 0, 0)),
                  _resident((1, d)), _resident((1, d))],
        out_specs=pl.BlockSpec((tm, d), lambda i, e: (i, 0)),
        scratch_shapes=[pltpu.VMEM((tm, d), BF16), pltpu.VMEM((tm, 128), F32), pltpu.VMEM((tm, d), F32)],
        compiler_params=_params(("parallel", "arbitrary")),
        name="moe_ln",
    )(x, w_router.T, router_bias.reshape(N_EXPERTS, 1), w1.astype(BF16), w3.astype(BF16), w2.astype(BF16),
      g.reshape(1, d), b.reshape(1, d))


def _na_bias_table(rpb, rows):
    assert rows >= NA_WIN_ROWS
    cols = np.arange(GRID_W)
    win = np.clip(cols - NA_WIN_COLS // 2, 0, GRID_W - NA_WIN_COLS)
    rel = cols[None, :] - cols[:, None]
    ok = (cols[None, :] >= win[:, None]) & (cols[None, :] < win[:, None] + NA_WIN_COLS)
    pick = (np.arange(2 * NA_WIN_COLS - 1)[None, None, :] == (rel + NA_WIN_COLS - 1)[:, :, None]) & ok[:, :, None]
    t = jnp.einsum("hdr,ckr->hdck", rpb.astype(F32), jnp.asarray(pick, F32), precision=lax.Precision.HIGHEST)
    t = jnp.where(ok[None, None], t, NEG)
    per_class = [t[:, NA_WIN_ROWS - 1 - cls:2 * NA_WIN_ROWS - 1 - cls] for cls in range(NA_ROW_CLASSES)]
    t = jnp.stack(per_class, axis=1)
    return jnp.transpose(t, (0, 1, 3, 2, 4)).reshape(rpb.shape[0], NA_ROW_CLASSES, GRID_W, NA_WIN_ROWS * GRID_W)


def _na_body(qkv_ref, bias_ref, o_ref):
    rows = qkv_ref.shape[1] // GRID_W
    band = NA_WIN_ROWS * GRID_W

    def row_step(r, carry):
        rs = jnp.clip(r - NA_WIN_ROWS // 2, 0, rows - NA_WIN_ROWS)
        cls = jnp.where(r < NA_WIN_ROWS // 2, r, NA_WIN_ROWS // 2 + jnp.maximum(r - (rows - NA_WIN_ROWS // 2), 0))
        q0 = pl.multiple_of(r * GRID_W, GRID_W)
        k0 = pl.multiple_of(rs * GRID_W, GRID_W)
        head_cols = lambda base, h: slice(base + h * HEAD_DIM, base + (h + 1) * HEAD_DIM)
        scores = [_dot_nt(qkv_ref[0, pl.ds(q0, GRID_W), head_cols(0, h)], qkv_ref[0, pl.ds(k0, band), head_cols(BRANCH_W, h)])
                  for h in range(N_HEADS)]
        probs, sums = [], []
        for h in range(N_HEADS):
            s = scores[h] * (HEAD_DIM ** -0.5) + bias_ref[h, cls]
            p = jnp.exp(s - jnp.max(s, axis=1, keepdims=True))
            probs.append(p.astype(BF16))
            sums.append(jnp.sum(p, axis=1, keepdims=True))
        outs = [jnp.dot(probs[h], qkv_ref[0, pl.ds(k0, band), head_cols(2 * BRANCH_W, h)], preferred_element_type=F32)
                for h in range(N_HEADS)]
        for h in range(N_HEADS):
            o_ref[0, pl.ds(q0, GRID_W), head_cols(0, h)] = (outs[h] / sums[h]).astype(BF16)
        return carry

    lax.fori_loop(0, rows, row_step, 0)


def _na_attention(qkv, bias):
    bsz, s, _ = qkv.shape
    return pl.pallas_call(
        _na_body,
        out_shape=jax.ShapeDtypeStruct((bsz, s, BRANCH_W), BF16),
        grid=(bsz,),
        in_specs=[pl.BlockSpec((1, s, W_NA), lambda b: (b, 0, 0)), _resident(bias.shape)],
        out_specs=pl.BlockSpec((1, s, BRANCH_W), lambda b: (b, 0, 0)),
        compiler_params=_params(("parallel",)),
        name="na_attention",
    )(qkv, bias)


def _eye(n):
    return lax.broadcasted_iota(jnp.int32, (n, n), 0) == lax.broadcasted_iota(jnp.int32, (n, n), 1)


def _row_to_col(row, eye):
    return jnp.sum(jnp.where(eye, row, 0.0), axis=1, keepdims=True)


def _conv3_rows(ref, c, nch, cols, w_ref):
    seq = nch * CHUNK
    r0 = pl.multiple_of(c * CHUNK, CHUNK)
    x = ref[0, pl.ds(r0, CHUNK), cols].astype(F32)
    lo = pl.multiple_of(jnp.maximum(r0 - 16, 0), 16)
    hi = pl.multiple_of(jnp.minimum(r0 + CHUNK, seq - 16), 16)
    prev = ref[0, pl.ds(lo, 16), cols][15:16, :].astype(F32) * (c > 0).astype(F32)
    nxt = ref[0, pl.ds(hi, 16), cols][0:1, :].astype(F32) * (c < nch - 1).astype(F32)
    rid = lax.broadcasted_iota(jnp.int32, x.shape, 0)
    xm = jnp.where(rid == 0, prev, pltpu.roll(x, 1, axis=0))
    xp = jnp.where(rid == CHUNK - 1, nxt, pltpu.roll(x, CHUNK - 1, axis=0))
    return w_ref[0:1, :] * xm + w_ref[1:2, :] * x + w_ref[2:3, :] * xp


def _head_norm(h, eps):
    mu = jnp.mean(h, axis=-1, keepdims=True)
    hc = h - mu
    return hc * lax.rsqrt(jnp.mean(hc * hc, axis=-1, keepdims=True) + eps)


def _dot_nt(a, b, **kw):
    return lax.dot_general(a, b, (((1,), (1,)), ((), ())), preferred_element_type=F32, **kw)


def _dot_tn(a, b, **kw):
    return lax.dot_general(a, b, (((0,), (0,)), ((), ())), preferred_element_type=F32, **kw)


N_STREAMS = 2 * N_HEADS


def _mlstm_body(qkv_ref, o_ref_in, g_ref, gb_ref, cw_ref, ng_ref, out_ref,
                qk_s, ig_s, b_s, ct_s, n_s, mc_s, bl_s, mprev_s, cst_s, nst_s, mst_s):
    seq = qkv_ref.shape[1]
    nch = seq // CHUNK
    eye = _eye(CHUNK)
    ri = lax.broadcasted_iota(jnp.int32, (CHUNK, CHUNK), 0)
    ci = lax.broadcasted_iota(jnp.int32, (CHUNK, CHUNK), 1)
    head_cols = lambda base, h: slice(base + h * HEAD_DIM, base + (h + 1) * HEAD_DIM)

    def conv_step(c, carry):
        y = _conv3_rows(qkv_ref, c, nch, slice(0, 2 * BRANCH_W), cw_ref)
        y = y * jax.nn.sigmoid(y)
        scale = jnp.where(lax.broadcasted_iota(jnp.int32, (1, 2 * BRANCH_W), 1) < BRANCH_W, 1.0, HEAD_DIM ** -0.5)
        qk_s[pl.ds(pl.multiple_of(c * CHUNK, CHUNK), CHUNK), :] = (y * scale).astype(BF16)
        return carry

    lax.fori_loop(0, nch, conv_step, 0)

    for j in range(N_STREAMS):
        rev, h = j >= N_HEADS, j % N_HEADS
        gi = (2 * N_HEADS if rev else 0) + h
        ig_s[j] = g_ref[0, gi] + gb_ref[gi]
        fpre = g_ref[0, gi + N_HEADS] + gb_ref[gi + N_HEADS]
        lf = jnp.minimum(fpre, 0.0) - jnp.log1p(jnp.exp(-jnp.abs(fpre)))
        cum = (ri >= ci) if rev else (ri <= ci)
        b_s[j] = jnp.dot(lf, cum.astype(F32), precision=lax.Precision.HIGHEST, preferred_element_type=F32)
        cst_s[j] = jnp.zeros((HEAD_DIM, HEAD_DIM), F32)
        nst_s[j] = jnp.zeros((1, HEAD_DIM), F32)
        mst_s[j] = jnp.full((1, 128), NEG, F32)

    def chunk_stats(c, carry):
        rows = pl.ds(pl.multiple_of(c * CHUNK, CHUNK), CHUNK)
        kws = []
        for j in range(N_STREAMS):
            rev, h = j >= N_HEADS, j % N_HEADS
            b_r = b_s[j, pl.ds(c, 1), :]
            bl = b_r[:, 0:1] if rev else b_r[:, CHUNK - 1:CHUNK]
            a_r = bl - b_r + ig_s[j, pl.ds(c, 1), :]
            mc = jnp.max(a_r, axis=1, keepdims=True)
            wa_c = _row_to_col(jnp.exp(a_r - mc), eye)
            kw = qk_s[rows, head_cols(BRANCH_W, h)].astype(F32) * wa_c
            kws.append(kw.astype(BF16))
            n_s[j, c] = jnp.sum(kw, axis=0, keepdims=True)
            mc_s[j, c] = jnp.broadcast_to(mc, (1, 128))
            bl_s[j, c] = jnp.broadcast_to(bl, (1, 128))
        for j in range(N_STREAMS):
            ct_s[j, c] = _dot_tn(kws[j], qkv_ref[0, rows, head_cols(2 * BRANCH_W, j % N_HEADS)])
        return carry

    lax.fori_loop(0, nch, chunk_stats, 0)

    def scan_step(i, carry):
        for j in range(N_STREAMS):
            c = (nch - 1 - i) if j >= N_HEADS else i
            ct, n, m = cst_s[j], nst_s[j], mst_s[j]
            bl, mc = bl_s[j, c], mc_s[j, c]
            m_new = jnp.maximum(bl + m, mc)
            sp = jnp.exp(bl + m - m_new)[:, :HEAD_DIM]
            sc = jnp.exp(mc - m_new)[:, :HEAD_DIM]
            cst_s[j] = sp * ct + sc * ct_s[j, c]
            nst_s[j] = sp * n + sc * n_s[j, c]
            mst_s[j] = m_new
            ct_s[j, c] = ct
            n_s[j, c] = n
            mprev_s[j, c] = m
        return carry

    lax.fori_loop(0, nch, scan_step, 0)

    def chunk_out(c, carry):
        rows = pl.ds(pl.multiple_of(c * CHUNK, CHUNK), CHUNK)
        qs = [qk_s[rows, head_cols(0, h)] for h in range(N_HEADS)]
        vs = [qkv_ref[0, rows, head_cols(2 * BRANCH_W, h)] for h in range(N_HEADS)]
        qks = [_dot_nt(qs[h], qk_s[rows, head_cols(BRANCH_W, h)]) for h in range(N_HEADS)]
        inter_parts = [jnp.dot(qs[j % N_HEADS], ct_s[j, c].astype(BF16), preferred_element_type=F32) for j in range(N_STREAMS)]
        stats = []
        for j in range(N_STREAMS):
            rev, h = j >= N_HEADS, j % N_HEADS
            tri = (ci >= ri) if rev else (ci <= ri)
            b_r = b_s[j, pl.ds(c, 1), :]
            b_c = _row_to_col(b_r, eye)
            dmat = jnp.where(tri, b_c - b_r + ig_s[j, pl.ds(c, 1), :], NEG)
            m_intra = jnp.max(dmat, axis=1, keepdims=True)
            g_c = b_c + mprev_s[j, c][:, 0:1]
            m_t = jnp.maximum(g_c, m_intra)
            stats.append((jnp.exp(g_c - m_t), qks[h] * jnp.exp(dmat - m_t), m_t))
        intra_parts = [jnp.dot(stats[j][1].astype(BF16), vs[j % N_HEADS], preferred_element_type=F32) for j in range(N_STREAMS)]
        hs = []
        for j in range(N_STREAMS):
            inter, w_intra, m_t = stats[j]
            num = inter * inter_parts[j] + intra_parts[j]
            den = inter * jnp.sum(qs[j % N_HEADS].astype(F32) * n_s[j, c], axis=1, keepdims=True) \
                + jnp.sum(w_intra, axis=1, keepdims=True)
            hs.append(num / jnp.maximum(jnp.abs(den), jnp.exp(-m_t)))
        for h in range(N_HEADS):
            hc = head_cols(0, h)
            tot = _head_norm(hs[h] + hs[N_HEADS + h], ML_NORM_EPS) * ng_ref[:, hc]
            out_ref[0, rows, hc] = (tot * jax.nn.sigmoid(o_ref_in[0, rows, hc].astype(F32))).astype(BF16)
        return carry

    lax.fori_loop(0, nch, chunk_out, 0)


def _mlstm(qkv, o_pre, small, conv_w, gate_b, norm_g):
    bsz, s, _ = qkv.shape
    nch = s // CHUNK
    n_gate = 4 * N_HEADS
    gates = small[..., SMALL_GATE_OFF:SMALL_GATE_OFF + n_gate].astype(F32)
    gates = jnp.transpose(gates, (0, 2, 1)).reshape(bsz, n_gate, nch, CHUNK)
    gate_b = jnp.broadcast_to(gate_b.astype(F32).reshape(n_gate, 1, 1), (n_gate, 1, CHUNK))
    row = lambda shape: pltpu.VMEM(shape, F32)
    return pl.pallas_call(
        _mlstm_body,
        out_shape=jax.ShapeDtypeStruct((bsz, s, BRANCH_W), BF16),
        grid=(bsz,),
        in_specs=[pl.BlockSpec((1, s, W_MLQKV), lambda b: (b, 0, 0)),
                  pl.BlockSpec((1, s, BRANCH_W), lambda b: (b, 0, 0)),
                  pl.BlockSpec((1, n_gate, nch, CHUNK), lambda b: (b, 0, 0, 0)),
                  _resident((n_gate, 1, CHUNK)), _resident((3, 2 * BRANCH_W)), _resident((1, BRANCH_W))],
        out_specs=pl.BlockSpec((1, s, BRANCH_W), lambda b: (b, 0, 0)),
        scratch_shapes=[pltpu.VMEM((s, 2 * BRANCH_W), BF16), row((N_STREAMS, nch, CHUNK)), row((N_STREAMS, nch, CHUNK)),
                        row((N_STREAMS, nch, HEAD_DIM, HEAD_DIM)), row((N_STREAMS, nch, 1, HEAD_DIM)),
                        row((N_STREAMS, nch, 1, 128)), row((N_STREAMS, nch, 1, 128)), row((N_STREAMS, nch, 1, 128)),
                        row((N_STREAMS, HEAD_DIM, HEAD_DIM)), row((N_STREAMS, 1, HEAD_DIM)), row((N_STREAMS, 1, 128))],
        compiler_params=_params(("parallel",)),
        name="mlstm",
    )(qkv, o_pre, gates, gate_b, conv_w.astype(F32), norm_g.astype(F32).reshape(1, BRANCH_W))


def _rwkv_body(rkv_ref, sm_ref, cw_ref, w0_ref, wup_ref, a0_ref, aup_ref, gup_ref, kk_ref, ka_ref, rk_ref, ng_ref, nb_ref,
               out_ref, y_s, bonus_s, gate_s, st_s):
    seq = rkv_ref.shape[1]
    nch = seq // CHUNK
    ri = lax.broadcasted_iota(jnp.int32, (CHUNK, CHUNK), 0)
    ci = lax.broadcasted_iota(jnp.int32, (CHUNK, CHUNK), 1)
    ident = (ri == ci).astype(F32)

    bdot = lambda x, y: jnp.dot(x.astype(BF16), y.astype(BF16), preferred_element_type=F32)

    def low_rank(x, d, up_ref, bias_ref):
        return bias_ref[d:d + 1, :] + bdot(x[:, d * RW_RANK:(d + 1) * RW_RANK], up_ref[d])

    st_s[...] = jnp.zeros_like(st_s)

    def chunk_step(i, carry):
        streams = []
        for rev in (False, True):
            d = 1 if rev else 0
            strict = (ci > ri) if rev else (ci < ri)
            incl = (ci >= ri) if rev else (ci <= ri)
            last = slice(0, 1) if rev else slice(CHUNK - 1, CHUNK)
            c = (nch - 1 - i) if rev else i
            rows = pl.ds(pl.multiple_of(c * CHUNK, CHUNK), CHUNK)
            rkv = _conv3_rows(rkv_ref, c, nch, slice(0, 3 * BRANCH_W), cw_ref)
            r, k, v = rkv[:, :BRANCH_W], rkv[:, BRANCH_W:2 * BRANCH_W], rkv[:, 2 * BRANCH_W:]
            sm = sm_ref[0, rows, :].astype(F32)
            w_lo, a_lo, g_lo = jnp.tanh(sm[:, :2 * RW_RANK]), sm[:, 2 * RW_RANK:4 * RW_RANK], sm[:, 4 * RW_RANK:6 * RW_RANK]
            lw = -RW_DECAY_SCALE * jax.nn.sigmoid(low_rank(w_lo, d, wup_ref, w0_ref))
            a = jax.nn.sigmoid(low_rank(a_lo, d, aup_ref, a0_ref))
            kd = k * (1.0 + (a - 1.0) * ka_ref[...])
            kk_raw = k * kk_ref[...]
            lw_hi = lw.astype(BF16)
            lw_lo = (lw - lw_hi.astype(F32)).astype(BF16)
            cum = jnp.dot(incl.astype(BF16), jnp.concatenate([lw_hi, lw_lo], axis=1), preferred_element_type=F32)
            lp_inc = cum[:, :BRANCH_W] + cum[:, BRANCH_W:]
            p_end = jnp.exp(lp_inc[last, :])
            e_inc = jnp.exp(lp_inc)
            e_inv = jnp.exp(-lp_inc)
            e_exc = jnp.exp(lp_inc - lw)
            if not rev:
                a_b = jax.nn.sigmoid(low_rank(a_lo, 1, aup_ref, a0_ref))
                kd_sum = kd + k * (1.0 + (a_b - 1.0) * ka_ref[...])
                gate_s[rows, :] = bdot(jax.nn.sigmoid(g_lo), gup_ref[...])
            for h in range(N_HEADS):
                hc = slice(h * HEAD_DIM, (h + 1) * HEAD_DIM)
                kk_h = kk_raw[:, hc]
                kk_h = kk_h / jnp.maximum(jnp.sqrt(jnp.sum(kk_h * kk_h, axis=1, keepdims=True)), 1e-12)
                streams.append(dict(
                    j=d * N_HEADS + h, d=d, rows=rows, hc=hc, strict=strict, incl=incl, p_end=p_end[:, hc],
                    b_hat=(kk_h * a[:, hc] * e_inv[:, hc]).astype(BF16), k_hat=(kd[:, hc] * e_inv[:, hc]).astype(BF16),
                    kk_t=kk_h * e_exc[:, hc], r_t=r[:, hc] * e_inc[:, hc], v_h=v[:, hc].astype(BF16)))
                if not rev:
                    bonus_s[rows, hc] = jnp.sum(r[:, hc] * kd_sum[:, hc] * rk_ref[:, hc], axis=1, keepdims=True) * v[:, hc]

        for s in streams:
            s["p4"] = _dot_nt(jnp.concatenate([s["kk_t"], s["r_t"]], axis=0).astype(BF16),
                              jnp.concatenate([s["b_hat"], s["k_hat"]], axis=0))
            s["nil"] = jnp.where(s["strict"], -s["p4"][:CHUNK, :CHUNK], 0.0)
            s["inv"] = ident + s["nil"]
        for _ in range(5):
            for s in streams:
                nil_b = s["nil"].astype(BF16)
                s["nil"] = jnp.dot(nil_b, nil_b, preferred_element_type=F32)
            for s in streams:
                s["inv"] = s["inv"] + bdot(s["inv"], s["nil"])
        for s in streams:
            p4 = s["p4"]
            masked = jnp.concatenate([jnp.where(s["strict"], p4[:CHUNK, CHUNK:], 0.0),
                                      jnp.where(s["incl"], p4[CHUNK:, CHUNK:], 0.0)], axis=0)
            s["akv_rkv"] = bdot(masked, s["v_h"])
        for s in streams:
            s["m12"] = bdot(s["inv"], jnp.concatenate([s["kk_t"], s["akv_rkv"][:CHUNK]], axis=1)).astype(BF16)
        for s in streams:
            s["gh"] = _dot_tn(s["m12"], s["b_hat"])
            s["vk"] = _dot_tn(s["v_h"], s["k_hat"])
        for s in streams:
            s["qy"] = bdot(jnp.where(s["incl"], s["p4"][CHUNK:, :CHUNK], 0.0), s["m12"])
        for s in streams:
            g_mat = (ident - s["gh"][:HEAD_DIM]) * s["p_end"]
            h_mat = (s["vk"] - s["gh"][HEAD_DIM:]) * s["p_end"]
            q_mat = s["r_t"] - s["qy"][:, :HEAD_DIM]
            y0 = s["akv_rkv"][CHUNK:] - s["qy"][:, HEAD_DIM:]
            s0 = st_s[s["j"]].astype(BF16)
            y_s[s["d"], s["rows"], s["hc"]] = _dot_nt(q_mat.astype(BF16), s0) + y0
            st_s[s["j"]] = bdot(s0, g_mat) + h_mat
        return carry

    lax.fori_loop(0, nch, chunk_step, 0)

    def finish(c, carry):
        rows = pl.ds(pl.multiple_of(c * CHUNK, CHUNK), CHUNK)
        for h in range(N_HEADS):
            hc = slice(h * HEAD_DIM, (h + 1) * HEAD_DIM)
            tot = _head_norm(y_s[0, rows, hc] + y_s[1, rows, hc], RW_NORM_EPS) * ng_ref[:, hc] + nb_ref[:, hc]
            out_ref[0, rows, hc] = ((tot + bonus_s[rows, hc]) * gate_s[rows, hc]).astype(BF16)
        return carry

    lax.fori_loop(0, nch, finish, 0)


def _rwkv(rkv, small, conv_w, w0, w_up, a0, a_up, g_up, k_k, k_a, r_k, norm_g, norm_b):
    bsz, s, _ = rkv.shape
    vec = lambda t: t.astype(F32).reshape(1, BRANCH_W)
    f = lambda t: t.astype(F32)
    return pl.pallas_call(
        _rwkv_body,
        out_shape=jax.ShapeDtypeStruct((bsz, s, BRANCH_W), BF16),
        grid=(bsz,),
        in_specs=[pl.BlockSpec((1, s, W_RKV), lambda b: (b, 0, 0)),
                  pl.BlockSpec((1, s, W_SMALL), lambda b: (b, 0, 0)),
                  _resident((3, 3 * BRANCH_W)), _resident((2, BRANCH_W)), _resident((2, RW_RANK, BRANCH_W)),
                  _resident((2, BRANCH_W)), _resident((2, RW_RANK, BRANCH_W)), _resident((2 * RW_RANK, BRANCH_W)),
                  _resident((1, BRANCH_W)), _resident((1, BRANCH_W)), _resident((1, BRANCH_W)),
                  _resident((1, BRANCH_W)), _resident((1, BRANCH_W))],
        out_specs=pl.BlockSpec((1, s, BRANCH_W), lambda b: (b, 0, 0)),
        scratch_shapes=[pltpu.VMEM((2, s, BRANCH_W), F32), pltpu.VMEM((s, BRANCH_W), F32), pltpu.VMEM((s, BRANCH_W), F32),
                        pltpu.VMEM((N_STREAMS, HEAD_DIM, HEAD_DIM), F32)],
        compiler_params=_params(("parallel",)),
        name="rwkv7",
    )(rkv, small, f(conv_w), f(w0), w_up.astype(BF16), f(a0), a_up.astype(BF16), g_up.astype(BF16), vec(k_k), vec(k_a),
      vec(r_k), vec(norm_g), vec(norm_b))


def _trunk(x, p, w_in_perm):
    bsz, s, d = x.shape
    n = bsz * s
    seq = lambda t: t.reshape(bsz, s, t.shape[-1])
    flat = lambda t: t.reshape(n, t.shape[-1])
    xf = _ln_rows(x.reshape(n, d), p["ln0_g"], p["ln0_b"])
    for l in range(DEPTH):
        na, mlqkv, mlo, rkv, small, merge = _in_proj(xf, w_in_perm[l])
        o_a = _na_attention(seq(na), _na_bias_table(p["na_rpb"][l], s // GRID_W))
        o_b = _mlstm(seq(mlqkv), seq(mlo), seq(small), p["ml_conv"][l], p["ml_gate_b"][l], p["ml_norm_g"][l])
        o_c = _rwkv(seq(rkv), seq(small), p["rw_conv"][l], p["rw_w0"][l], p["rw_w_up"][l], p["rw_a0"][l], p["rw_a_up"][l],
                    p["rw_g_up"][l], p["rw_k_k"][l], p["rw_k_a"][l], p["rw_r_k"][l], p["rw_norm_g"][l], p["rw_norm_b"][l])
        xf = _mix_ln(xf, flat(o_a), flat(o_b), flat(o_c), merge, p["w_br_a"][l], p["w_br_b"][l], p["w_br_c"][l],
                     p["w_out"][l], p["ln1_g"][l], p["ln1_b"][l])
        xf = _moe_ln(xf, p["w_router"], p["router_bias"], p["moe_w1"][l], p["moe_w3"][l], p["moe_w2"][l],
                     p["ln2_g"][l], p["ln2_b"][l])
    return xf.reshape(bsz, s, d)


def kernel(x_prompt, x_sample, ln0_g, ln0_b, w_in, na_rpb, ml_conv, ml_gate_b, ml_norm_g, rw_conv, rw_w0, rw_w_up,
           rw_a0, rw_a_up, rw_g_up, rw_k_k, rw_k_a, rw_r_k, rw_norm_g, rw_norm_b, w_br_a, w_br_b, w_br_c, w_out,
           ln1_g, ln1_b, w_router, router_bias, moe_w1, moe_w3, moe_w2, ln2_g, ln2_b):
    p = {
        "ln0_g": ln0_g, "ln0_b": ln0_b, "na_rpb": na_rpb, "ml_conv": ml_conv,
        "ml_gate_b": ml_gate_b, "ml_norm_g": ml_norm_g, "rw_conv": rw_conv, "rw_w0": rw_w0,
        "rw_w_up": rw_w_up, "rw_a0": rw_a0, "rw_a_up": rw_a_up, "rw_g_up": rw_g_up, "rw_k_k": rw_k_k,
        "rw_k_a": rw_k_a, "rw_r_k": rw_r_k, "rw_norm_g": rw_norm_g, "rw_norm_b": rw_norm_b,
        "w_br_a": w_br_a, "w_br_b": w_br_b, "w_br_c": w_br_c, "w_out": w_out, "ln1_g": ln1_g,
        "ln1_b": ln1_b, "w_router": w_router, "router_bias": router_bias, "moe_w1": moe_w1,
        "moe_w3": moe_w3, "moe_w2": moe_w2, "ln2_g": ln2_g, "ln2_b": ln2_b,
    }
    w_in_perm = [_permute_w_in(w_in[l]) for l in range(DEPTH)]
    return _trunk(x_prompt, p, w_in_perm), _trunk(x_sample, p, w_in_perm)
```

```python
import functools

import jax
import jax.numpy as jnp
import numpy as np
from jax import lax
from jax.experimental import pallas as pl
from jax.experimental.pallas import tpu as pltpu

F32 = jnp.float32
BF16 = jnp.bfloat16

D_MODEL = 1024
DEPTH = 4
GRID_W = 64
HEAD_DIM = 64
N_HEADS = 4
BRANCH_W = N_HEADS * HEAD_DIM
CHUNK = 64
NA_WIN_ROWS = 8
NA_WIN_COLS = 16
NA_ROW_CLASSES = 8
ML_NORM_EPS = 1e-6
RW_RANK = 32
RW_DECAY_SCALE = 0.606531
RW_NORM_EPS = 64e-5
N_EXPERTS = 16
EXPERTS_PER_GROUP = 4
D_EXPERT = 512
ALPHA = (2 * DEPTH) ** 0.25
LN_EPS = 1e-5
NEG = -1e30

W_NA = 3 * BRANCH_W
W_MLQKV = 3 * BRANCH_W
W_MLO = BRANCH_W
W_RKV = 3 * BRANCH_W
W_SMALL = 256
W_MERGE = 3 * D_MODEL
PIECES = (W_NA, W_MLQKV, W_MLO, W_RKV, W_SMALL, W_MERGE)
D_INP = sum(PIECES)
SMALL_GATE_OFF = 192

VMEM_LIMIT = 56 * 1024 * 1024
TOKEN_TILE = 512


def _params(sem):
    return pltpu.CompilerParams(dimension_semantics=sem, vmem_limit_bytes=VMEM_LIMIT)


def _resident(shape):
    nd = len(shape)
    return pl.BlockSpec(shape, lambda *_: (0,) * nd, pipeline_mode=pl.Buffered(1))


def _ln(z, g, b):
    mu = jnp.mean(z, axis=-1, keepdims=True)
    zc = z - mu
    var = jnp.mean(zc * zc, axis=-1, keepdims=True)
    return zc * lax.rsqrt(var + LN_EPS) * g + b


def _ln_rows_body(x_ref, g_ref, b_ref, o_ref):
    o_ref[...] = _ln(x_ref[...], g_ref[...], b_ref[...])


def _ln_rows(x, g, b):
    n, d = x.shape
    tm = TOKEN_TILE
    return pl.pallas_call(
        _ln_rows_body,
        out_shape=jax.ShapeDtypeStruct((n, d), F32),
        grid=(n // tm,),
        in_specs=[pl.BlockSpec((tm, d), lambda i: (i, 0)), _resident((1, d)), _resident((1, d))],
        out_specs=pl.BlockSpec((tm, d), lambda i: (i, 0)),
        compiler_params=_params(("parallel",)),
        name="ln_rows",
    )(x, g.reshape(1, d), b.reshape(1, d))


def _in_proj_body(x_ref, w_ref, *o_refs):
    xb = x_ref[...].astype(BF16)
    off = 0
    for o_ref, width in zip(o_refs, PIECES):
        for c0 in range(0, width, 768):
            c1 = min(c0 + 768, width)
            o_ref[:, c0:c1] = jnp.dot(xb, w_ref[:, off + c0:off + c1], preferred_element_type=F32).astype(BF16)
        off += width


def _in_proj(x, w):
    n, d = x.shape
    tm = TOKEN_TILE
    return pl.pallas_call(
        _in_proj_body,
        out_shape=[jax.ShapeDtypeStruct((n, wd), BF16) for wd in PIECES],
        grid=(n // tm,),
        in_specs=[pl.BlockSpec((tm, d), lambda i: (i, 0)), _resident((d, D_INP))],
        out_specs=[pl.BlockSpec((tm, wd), lambda i: (i, 0)) for wd in PIECES],
        compiler_params=_params(("parallel",)),
        name="in_proj",
    )(x, w)


def _permute_w_in(w_in_l):
    na, mlqkv, mlo, mlg, rkv, wl, al, gl, merge = jnp.split(
        w_in_l, np.cumsum([768, 768, 256, 16, 768, 64, 64, 64])[:8].tolist(), axis=1)
    pad = jnp.zeros((w_in_l.shape[0], W_SMALL - 208), w_in_l.dtype)
    return jnp.concatenate([na, mlqkv, mlo, rkv, wl, al, gl, mlg, pad, merge], axis=1).astype(BF16)


def _mix_ln_body(x_ref, oa_ref, ob_ref, oc_ref, mg_ref, wa_ref, wb_ref, wc_ref, wo_ref, g_ref, b_ref, o_ref):
    mixed = None
    for j, (o_br, w_br) in enumerate(((oa_ref, wa_ref), (ob_ref, wb_ref), (oc_ref, wc_ref))):
        gate = jax.nn.sigmoid(mg_ref[:, j * D_MODEL:(j + 1) * D_MODEL].astype(F32))
        term = gate * jnp.dot(o_br[...], w_br[...], preferred_element_type=F32)
        mixed = term if mixed is None else mixed + term
    z = ALPHA * x_ref[...] + jnp.dot(mixed.astype(BF16), wo_ref[...], preferred_element_type=F32)
    o_ref[...] = _ln(z, g_ref[...], b_ref[...])


def _mix_ln(x, oa, ob, oc, merge, wa, wb, wc, wo, g, b):
    n, d = x.shape
    tm = TOKEN_TILE
    row = lambda wd: pl.BlockSpec((tm, wd), lambda i: (i, 0))
    return pl.pallas_call(
        _mix_ln_body,
        out_shape=jax.ShapeDtypeStruct((n, d), F32),
        grid=(n // tm,),
        in_specs=[row(d), row(BRANCH_W), row(BRANCH_W), row(BRANCH_W), row(W_MERGE),
                  _resident((BRANCH_W, d)), _resident((BRANCH_W, d)), _resident((BRANCH_W, d)), _resident((d, d)),
                  _resident((1, d)), _resident((1, d))],
        out_specs=row(d),
        compiler_params=_params(("parallel",)),
        name="mix_ln",
    )(x, oa, ob, oc, merge, wa.astype(BF16), wb.astype(BF16), wc.astype(BF16), wo.astype(BF16),
      g.reshape(1, d), b.reshape(1, d))


MOE_TILE = 1024


def _top2_sum(a, b, c, d):
    hi1, lo1 = jnp.maximum(a, b), jnp.minimum(a, b)
    hi2, lo2 = jnp.maximum(c, d), jnp.minimum(c, d)
    return jnp.maximum(hi1, hi2) + jnp.maximum(jnp.minimum(hi1, hi2), jnp.maximum(lo1, lo2))


def _router_gates(x, wrt, rbias):
    logits = lax.dot_general(wrt, x, (((1,), (1,)), ((), ())), precision=lax.Precision.HIGHEST,
                             preferred_element_type=F32)
    ex = jnp.exp(logits - jnp.max(logits, axis=0, keepdims=True))
    probs = ex / jnp.sum(ex, axis=0, keepdims=True)
    sel = probs + rbias
    rows = [sel[e:e + 1, :] for e in range(N_EXPERTS)]
    n_groups = N_EXPERTS // EXPERTS_PER_GROUP
    scores = [_top2_sum(*rows[EXPERTS_PER_GROUP * g:EXPERTS_PER_GROUP * (g + 1)]) for g in range(n_groups)]
    best, gidx = scores[0], jnp.zeros_like(scores[0], dtype=jnp.int32)
    for g in range(1, n_groups):
        better = scores[g] > best
        gidx = jnp.where(better, g, gidx)
        best = jnp.where(better, scores[g], best)
    picked = []
    for e in range(N_EXPERTS):
        g = e // EXPERTS_PER_GROUP
        rank = jnp.zeros_like(gidx)
        for e2 in range(EXPERTS_PER_GROUP * g, EXPERTS_PER_GROUP * (g + 1)):
            if e2 == e:
                continue
            ahead = (rows[e2] >= rows[e]) if e2 < e else (rows[e2] > rows[e])
            rank = rank + ahead.astype(jnp.int32)
        chosen = (gidx == g) & (rank < 2)
        picked.append(jnp.where(chosen, probs[e:e + 1, :], 0.0))
    total = picked[0]
    for p in picked[1:]:
        total = total + p
    return jnp.concatenate(picked, axis=0) / total


def _moe_ln_body(x_ref, wrt_ref, rb_ref, w1_ref, w3_ref, w2_ref, g_ref, b_ref, o_ref, xb_ref, gate_ref, acc_ref):
    e = pl.program_id(1)
    tm = x_ref.shape[0]

    @pl.when(e == 0)
    def _():
        x = x_ref[...]
        xb_ref[...] = x.astype(BF16)
        gates = _router_gates(x, wrt_ref[...], rb_ref[...])
        padded = jnp.concatenate([gates, jnp.zeros((128 - N_EXPERTS, tm), F32)], axis=0)
        gate_ref[...] = padded.T
        acc_ref[...] = jnp.zeros_like(acc_ref)

    xb = xb_ref[...]
    h1 = jnp.dot(xb, w1_ref[0], preferred_element_type=F32)
    h3 = jnp.dot(xb, w3_ref[0], preferred_element_type=F32)
    h = (h1 * jax.nn.sigmoid(h1) * h3).astype(BF16)
    y = jnp.dot(h, w2_ref[0], preferred_element_type=F32)
    lane = lax.broadcasted_iota(jnp.int32, (tm, 128), 1)
    gate_e = jnp.sum(jnp.where(lane == e, gate_ref[...], 0.0), axis=1, keepdims=True)
    acc_ref[...] += gate_e * y

    @pl.when(e == N_EXPERTS - 1)
    def _():
        o_ref[...] = _ln(ALPHA * x_ref[...] + acc_ref[...], g_ref[...], b_ref[...])


def _moe_ln(x, w_router, router_bias, w1, w3, w2, g, b):
    n, d = x.shape
    tm = MOE_TILE
    return pl.pallas_call(
        _moe_ln_body,
        out_shape=jax.ShapeDtypeStruct((n, d), F32),
        grid=(n // tm, N_EXPERTS),
        in_specs=[pl.BlockSpec((tm, d), lambda i, e: (i, 0)),
                  _resident((N_EXPERTS, d)), _resident((N_EXPERTS, 1)),
                  pl.BlockSpec((1, d, D_EXPERT), lambda i, e: (e, 0, 0)),
                  pl.BlockSpec((1, d, D_EXPERT), lambda i, e: (e, 0, 0)),
                  pl.BlockSpec((1, D_EXPERT, d), lambda i, e: (e, 0, 0)),
                  _resident((1, d)), _resident((1, d))],
        out_specs=pl.BlockSpec((tm, d), lambda i, e: (i, 0)),
        scratch_shapes=[pltpu.VMEM((tm, d), BF16), pltpu.VMEM((tm, 128), F32), pltpu.VMEM((tm, d), F32)],
        compiler_params=_params(("parallel", "arbitrary")),
        name="moe_ln",
    )(x, w_router.T, router_bias.reshape(N_EXPERTS, 1), w1.astype(BF16), w3.astype(BF16), w2.astype(BF16),
      g.reshape(1, d), b.reshape(1, d))


def _na_bias_table(rpb, rows):
    assert rows >= NA_WIN_ROWS
    cols = np.arange(GRID_W)
    win = np.clip(cols - NA_WIN_COLS // 2, 0, GRID_W - NA_WIN_COLS)
    rel = cols[None, :] - cols[:, None]
    ok = (cols[None, :] >= win[:, None]) & (cols[None, :] < win[:, None] + NA_WIN_COLS)
    pick = (np.arange(2 * NA_WIN_COLS - 1)[None, None, :] == (rel + NA_WIN_COLS - 1)[:, :, None]) & ok[:, :, None]
    t = jnp.einsum("hdr,ckr->hdck", rpb.astype(F32), jnp.asarray(pick, F32), precision=lax.Precision.HIGHEST)
    t = jnp.where(ok[None, None], t, NEG)
    per_class = [t[:, NA_WIN_ROWS - 1 - cls:2 * NA_WIN_ROWS - 1 - cls] for cls in range(NA_ROW_CLASSES)]
    t = jnp.stack(per_class, axis=1)
    return jnp.transpose(t, (0, 1, 3, 2, 4)).reshape(rpb.shape[0], NA_ROW_CLASSES, GRID_W, NA_WIN_ROWS * GRID_W)


def _na_body(qkv_ref, bias_ref, o_ref):
    rows = qkv_ref.shape[1] // GRID_W
    band = NA_WIN_ROWS * GRID_W

    def row_step(r, carry):
        rs = jnp.clip(r - NA_WIN_ROWS // 2, 0, rows - NA_WIN_ROWS)
        cls = jnp.where(r < NA_WIN_ROWS // 2, r, NA_WIN_ROWS // 2 + jnp.maximum(r - (rows - NA_WIN_ROWS // 2), 0))
        q0 = pl.multiple_of(r * GRID_W, GRID_W)
        k0 = pl.multiple_of(rs * GRID_W, GRID_W)
        head_cols = lambda base, h: slice(base + h * HEAD_DIM, base + (h + 1) * HEAD_DIM)
        scores = [_dot_nt(qkv_ref[0, pl.ds(q0, GRID_W), head_cols(0, h)], qkv_ref[0, pl.ds(k0, band), head_cols(BRANCH_W, h)])
                  for h in range(N_HEADS)]
        probs, sums = [], []
        for h in range(N_HEADS):
            s = scores[h] * (HEAD_DIM ** -0.5) + bias_ref[h, cls]
            p = jnp.exp(s - jnp.max(s, axis=1, keepdims=True))
            probs.append(p.astype(BF16))
            sums.append(jnp.sum(p, axis=1, keepdims=True))
        outs = [jnp.dot(probs[h], qkv_ref[0, pl.ds(k0, band), head_cols(2 * BRANCH_W, h)], preferred_element_type=F32)
                for h in range(N_HEADS)]
        for h in range(N_HEADS):
            o_ref[0, pl.ds(q0, GRID_W), head_cols(0, h)] = (outs[h] / sums[h]).astype(BF16)
        return carry

    lax.fori_loop(0, rows, row_step, 0)


def _na_attention(qkv, bias):
    bsz, s, _ = qkv.shape
    return pl.pallas_call(
        _na_body,
        out_shape=jax.ShapeDtypeStruct((bsz, s, BRANCH_W), BF16),
        grid=(bsz,),
        in_specs=[pl.BlockSpec((1, s, W_NA), lambda b: (b, 0, 0)), _resident(bias.shape)],
        out_specs=pl.BlockSpec((1, s, BRANCH_W), lambda b: (b, 0, 0)),
        compiler_params=_params(("parallel",)),
        name="na_attention",
    )(qkv, bias)


def _eye(n):
    return lax.broadcasted_iota(jnp.int32, (n, n), 0) == lax.broadcasted_iota(jnp.int32, (n, n), 1)


def _row_to_col(row, eye):
    return jnp.sum(jnp.where(eye, row, 0.0), axis=1, keepdims=True)


def _conv3_rows(ref, c, nch, cols, w_ref):
    seq = nch * CHUNK
    r0 = pl.multiple_of(c * CHUNK, CHUNK)
    x = ref[0, pl.ds(r0, CHUNK), cols].astype(F32)
    lo = pl.multiple_of(jnp.maximum(r0 - 16, 0), 16)
    hi = pl.multiple_of(jnp.minimum(r0 + CHUNK, seq - 16), 16)
    prev = ref[0, pl.ds(lo, 16), cols][15:16, :].astype(F32) * jnp.where(c > 0, 1.0, 0.0)
    nxt = ref[0, pl.ds(hi, 16), cols][0:1, :].astype(F32) * jnp.where(c < nch - 1, 1.0, 0.0)
    rid = lax.broadcasted_iota(jnp.int32, x.shape, 0)
    xm = jnp.where(rid == 0, prev, pltpu.roll(x, 1, axis=0))
    xp = jnp.where(rid == CHUNK - 1, nxt, pltpu.roll(x, CHUNK - 1, axis=0))
    return w_ref[0:1, :] * xm + w_ref[1:2, :] * x + w_ref[2:3, :] * xp


def _head_norm(h, eps):
    mu = jnp.mean(h, axis=-1, keepdims=True)
    hc = h - mu
    return hc * lax.rsqrt(jnp.mean(hc * hc, axis=-1, keepdims=True) + eps)


def _dot_nt(a, b, **kw):
    return lax.dot_general(a, b, (((1,), (1,)), ((), ())), preferred_element_type=F32, **kw)


def _dot_tn(a, b, **kw):
    return lax.dot_general(a, b, (((0,), (0,)), ((), ())), preferred_element_type=F32, **kw)


N_STREAMS = 2 * N_HEADS


STATE_W = 2 * HEAD_DIM


def _mlstm_body(qkv_ref, o_ref_in, g_ref, gb_ref, cw_ref, ng_ref, out_ref,
                qk_s, vx_s, ig_s, b_s, ct_s, mc_s, bl_s, mprev_s, cst_s, mst_s):
    seq = qkv_ref.shape[1]
    nch = seq // CHUNK
    eye = _eye(CHUNK)
    ri = lax.broadcasted_iota(jnp.int32, (CHUNK, CHUNK), 0)
    ci = lax.broadcasted_iota(jnp.int32, (CHUNK, CHUNK), 1)
    head_cols = lambda base, h: slice(base + h * HEAD_DIM, base + (h + 1) * HEAD_DIM)
    ones_col = jnp.where(lax.broadcasted_iota(jnp.int32, (CHUNK, HEAD_DIM), 1) == 0, 1.0, 0.0).astype(BF16)

    def conv_step(c, carry):
        rows = pl.ds(pl.multiple_of(c * CHUNK, CHUNK), CHUNK)
        y = _conv3_rows(qkv_ref, c, nch, slice(0, 2 * BRANCH_W), cw_ref)
        y = y * jax.nn.sigmoid(y)
        scale = jnp.where(lax.broadcasted_iota(jnp.int32, (1, 2 * BRANCH_W), 1) < BRANCH_W, 1.0, HEAD_DIM ** -0.5)
        qk_s[rows, :] = (y * scale).astype(BF16)
        for h in range(N_HEADS):
            vx_s[rows, h * STATE_W:h * STATE_W + HEAD_DIM] = qkv_ref[0, rows, head_cols(2 * BRANCH_W, h)]
            vx_s[rows, h * STATE_W + HEAD_DIM:(h + 1) * STATE_W] = ones_col
        return carry

    lax.fori_loop(0, nch, conv_step, 0)

    for j in range(N_STREAMS):
        rev, h = j >= N_HEADS, j % N_HEADS
        gi = (2 * N_HEADS if rev else 0) + h
        ig_s[j] = g_ref[0, gi] + gb_ref[gi]
        fpre = g_ref[0, gi + N_HEADS] + gb_ref[gi + N_HEADS]
        lf = jnp.minimum(fpre, 0.0) - jnp.log1p(jnp.exp(-jnp.abs(fpre)))
        cum = (ri >= ci) if rev else (ri <= ci)
        b_s[j] = jnp.dot(lf, cum.astype(F32), precision=lax.Precision.HIGHEST, preferred_element_type=F32)
        cst_s[j] = jnp.zeros((HEAD_DIM, STATE_W), F32)
        mst_s[j] = jnp.full((1, 128), NEG, F32)

    def chunk_stats(c, carry):
        rows = pl.ds(pl.multiple_of(c * CHUNK, CHUNK), CHUNK)
        a_rs, mcs = [], []
        for j in range(N_STREAMS):
            b_r = b_s[j, pl.ds(c, 1), :]
            bl = b_r[:, 0:1] if j >= N_HEADS else b_r[:, CHUNK - 1:CHUNK]
            a_rs.append(bl - b_r + ig_s[j, pl.ds(c, 1), :])
            bl_s[j, c] = jnp.broadcast_to(bl, (1, 128))
        for j in range(N_STREAMS):
            mcs.append(jnp.max(a_rs[j], axis=1, keepdims=True))
            mc_s[j, c] = jnp.broadcast_to(mcs[j], (1, 128))
        wa_cs = [_row_to_col(jnp.exp(a_rs[j] - mcs[j]), eye) for j in range(N_STREAMS)]
        kws = [(qk_s[rows, head_cols(BRANCH_W, j % N_HEADS)].astype(F32) * wa_cs[j]).astype(BF16) for j in range(N_STREAMS)]
        for j in range(N_STREAMS):
            h = j % N_HEADS
            ct_s[j, c] = _dot_tn(kws[j], vx_s[rows, h * STATE_W:(h + 1) * STATE_W])
        return carry

    lax.fori_loop(0, nch, chunk_stats, 0)

    def scan_step(i, carry):
        for j in range(N_STREAMS):
            c = (nch - 1 - i) if j >= N_HEADS else i
            ct, m = cst_s[j], mst_s[j]
            bl, mc = bl_s[j, c], mc_s[j, c]
            m_new = jnp.maximum(bl + m, mc)
            cst_s[j] = jnp.exp(bl + m - m_new) * ct + jnp.exp(mc - m_new) * ct_s[j, c]
            mst_s[j] = m_new
            ct_s[j, c] = ct
            mprev_s[j, c] = m
        return carry

    lax.fori_loop(0, nch, scan_step, 0)

    def chunk_out(c, carry):
        rows = pl.ds(pl.multiple_of(c * CHUNK, CHUNK), CHUNK)
        qs = [qk_s[rows, head_cols(0, h)] for h in range(N_HEADS)]
        vs = [vx_s[rows, h * STATE_W:(h + 1) * STATE_W] for h in range(N_HEADS)]
        qks = [_dot_nt(qs[h], qk_s[rows, head_cols(BRANCH_W, h)]) for h in range(N_HEADS)]
        inter_parts = [jnp.dot(qs[j % N_HEADS], ct_s[j, c].astype(BF16), preferred_element_type=F32) for j in range(N_STREAMS)]
        b_rs = [b_s[j, pl.ds(c, 1), :] for j in range(N_STREAMS)]
        b_cs = [_row_to_col(b_rs[j], eye) for j in range(N_STREAMS)]
        dmats = []
        for j in range(N_STREAMS):
            tri = (ci >= ri) if j >= N_HEADS else (ci <= ri)
            dmats.append(jnp.where(tri, b_cs[j] - b_rs[j] + ig_s[j, pl.ds(c, 1), :], NEG))
        m_intras = [jnp.max(dmats[j], axis=1, keepdims=True) for j in range(N_STREAMS)]
        stats = []
        for j in range(N_STREAMS):
            g_c = b_cs[j] + mprev_s[j, c][:, 0:1]
            m_t = jnp.maximum(g_c, m_intras[j])
            stats.append((jnp.exp(g_c - m_t), qks[j % N_HEADS] * jnp.exp(dmats[j] - m_t), m_t))
        intra_parts = [jnp.dot(stats[j][1].astype(BF16), vs[j % N_HEADS], preferred_element_type=F32) for j in range(N_STREAMS)]
        hs = []
        for j in range(N_STREAMS):
            inter, _, m_t = stats[j]
            both = inter * inter_parts[j] + intra_parts[j]
            den = both[:, HEAD_DIM:HEAD_DIM + 1]
            hs.append(both[:, :HEAD_DIM] / jnp.maximum(jnp.abs(den), jnp.exp(-m_t)))
        for h in range(N_HEADS):
            hc = head_cols(0, h)
            tot = _head_norm(hs[h] + hs[N_HEADS + h], ML_NORM_EPS) * ng_ref[:, hc]
            out_ref[0, rows, hc] = (tot * jax.nn.sigmoid(o_ref_in[0, rows, hc].astype(F32))).astype(BF16)
        return carry

    lax.fori_loop(0, nch, chunk_out, 0)


def _mlstm(qkv, o_pre, small, conv_w, gate_b, norm_g):
    bsz, s, _ = qkv.shape
    nch = s // CHUNK
    n_gate = 4 * N_HEADS
    gates = small[..., SMALL_GATE_OFF:SMALL_GATE_OFF + n_gate].astype(F32)
    gates = jnp.transpose(gates, (0, 2, 1)).reshape(bsz, n_gate, nch, CHUNK)
    gate_b = jnp.broadcast_to(gate_b.astype(F32).reshape(n_gate, 1, 1), (n_gate, 1, CHUNK))
    row = lambda shape: pltpu.VMEM(shape, F32)
    return pl.pallas_call(
        _mlstm_body,
        out_shape=jax.ShapeDtypeStruct((bsz, s, BRANCH_W), BF16),
        grid=(bsz,),
        in_specs=[pl.BlockSpec((1, s, W_MLQKV), lambda b: (b, 0, 0)),
                  pl.BlockSpec((1, s, BRANCH_W), lambda b: (b, 0, 0)),
                  pl.BlockSpec((1, n_gate, nch, CHUNK), lambda b: (b, 0, 0, 0)),
                  _resident((n_gate, 1, CHUNK)), _resident((3, 2 * BRANCH_W)), _resident((1, BRANCH_W))],
        out_specs=pl.BlockSpec((1, s, BRANCH_W), lambda b: (b, 0, 0)),
        scratch_shapes=[pltpu.VMEM((s, 2 * BRANCH_W), BF16), pltpu.VMEM((s, N_HEADS * STATE_W), BF16),
                        row((N_STREAMS, nch, CHUNK)), row((N_STREAMS, nch, CHUNK)),
                        row((N_STREAMS, nch, HEAD_DIM, STATE_W)),
                        row((N_STREAMS, nch, 1, 128)), row((N_STREAMS, nch, 1, 128)), row((N_STREAMS, nch, 1, 128)),
                        row((N_STREAMS, HEAD_DIM, STATE_W)), row((N_STREAMS, 1, 128))],
        compiler_params=_params(("parallel",)),
        name="mlstm",
    )(qkv, o_pre, gates, gate_b, conv_w.astype(F32), norm_g.astype(F32).reshape(1, BRANCH_W))


RW_CHUNKS_PER_STEP = 2


def _rwkv_body(rkv_ref, sm_ref, cw_ref, w0_ref, wup_ref, a0_ref, aup_ref, gup_ref, kk_ref, ka_ref, rk_ref, ng_ref, nb_ref,
               out_ref, y_s, bonus_s, gate_s, st_s):
    seq = rkv_ref.shape[1]
    nch = seq // CHUNK
    ri = lax.broadcasted_iota(jnp.int32, (CHUNK, CHUNK), 0)
    ci = lax.broadcasted_iota(jnp.int32, (CHUNK, CHUNK), 1)
    ident = (ri == ci).astype(F32)

    bdot = lambda x, y: jnp.dot(x.astype(BF16), y.astype(BF16), preferred_element_type=F32)

    def low_rank(x, d, up_ref, bias_ref):
        return bias_ref[d:d + 1, :] + bdot(x[:, d * RW_RANK:(d + 1) * RW_RANK], up_ref[d])

    st_s[...] = jnp.zeros_like(st_s)

    def chunk_step(i, carry):
        streams = []
        for rev, u in ((False, 0), (False, 1), (True, 0), (True, 1)):
            d = 1 if rev else 0
            strict = (ci > ri) if rev else (ci < ri)
            incl = (ci >= ri) if rev else (ci <= ri)
            last = slice(0, 1) if rev else slice(CHUNK - 1, CHUNK)
            c = (nch - 1 - (RW_CHUNKS_PER_STEP * i + u)) if rev else (RW_CHUNKS_PER_STEP * i + u)
            rows = pl.ds(pl.multiple_of(c * CHUNK, CHUNK), CHUNK)
            rkv = _conv3_rows(rkv_ref, c, nch, slice(0, 3 * BRANCH_W), cw_ref)
            r, k, v = rkv[:, :BRANCH_W], rkv[:, BRANCH_W:2 * BRANCH_W], rkv[:, 2 * BRANCH_W:]
            sm = sm_ref[0, rows, :].astype(F32)
            w_lo, a_lo, g_lo = jnp.tanh(sm[:, :2 * RW_RANK]), sm[:, 2 * RW_RANK:4 * RW_RANK], sm[:, 4 * RW_RANK:6 * RW_RANK]
            lw = -RW_DECAY_SCALE * jax.nn.sigmoid(low_rank(w_lo, d, wup_ref, w0_ref))
            a = jax.nn.sigmoid(low_rank(a_lo, d, aup_ref, a0_ref))
            kd = k * (1.0 + (a - 1.0) * ka_ref[...])
            kk_raw = k * kk_ref[...]
            lw_hi = lw.astype(BF16)
            lw_lo = (lw - lw_hi.astype(F32)).astype(BF16)
            cum = jnp.dot(incl.astype(BF16), jnp.concatenate([lw_hi, lw_lo], axis=1), preferred_element_type=F32)
            lp_inc = cum[:, :BRANCH_W] + cum[:, BRANCH_W:]
            p_end = jnp.exp(lp_inc[last, :])
            e_inc = jnp.exp(lp_inc)
            e_inv = jnp.exp(-lp_inc)
            e_exc = jnp.exp(lp_inc - lw)
            if not rev:
                a_b = jax.nn.sigmoid(low_rank(a_lo, 1, aup_ref, a0_ref))
                kd_sum = kd + k * (1.0 + (a_b - 1.0) * ka_ref[...])
                gate_s[rows, :] = bdot(jax.nn.sigmoid(g_lo), gup_ref[...])
            for h in range(N_HEADS):
                hc = slice(h * HEAD_DIM, (h + 1) * HEAD_DIM)
                kk_h = kk_raw[:, hc]
                kk_h = kk_h / jnp.maximum(jnp.sqrt(jnp.sum(kk_h * kk_h, axis=1, keepdims=True)), 1e-12)
                streams.append(dict(
                    j=d * N_HEADS + h, d=d, rows=rows, hc=hc, strict=strict, incl=incl, p_end=p_end[:, hc],
                    b_hat=(kk_h * a[:, hc] * e_inv[:, hc]).astype(BF16), k_hat=(kd[:, hc] * e_inv[:, hc]).astype(BF16),
                    kk_t=kk_h * e_exc[:, hc], r_t=r[:, hc] * e_inc[:, hc], v_h=v[:, hc].astype(BF16)))
                if not rev:
                    bonus_s[rows, hc] = jnp.sum(r[:, hc] * kd_sum[:, hc] * rk_ref[:, hc], axis=1, keepdims=True) * v[:, hc]

        for s in streams:
            s["p4"] = _dot_nt(jnp.concatenate([s["kk_t"], s["r_t"]], axis=0).astype(BF16),
                              jnp.concatenate([s["b_hat"], s["k_hat"]], axis=0))
            s["pw"] = [jnp.where(s["strict"], -s["p4"][:CHUNK, :CHUNK], 0.0).astype(BF16)]
            s["pair"] = []
        for step in range(1, 6):
            for s in streams:
                s["pw"].append(jnp.dot(s["pw"][-1], s["pw"][-1], preferred_element_type=F32).astype(BF16))
            if step % 2 == 1:
                for s in streams:
                    lo, hi = s["pw"][step - 1], s["pw"][step]
                    s["pair"].append(ident + lo.astype(F32) + hi.astype(F32) + jnp.dot(lo, hi, preferred_element_type=F32))
        for s in streams:
            s["inv"] = bdot(s["pair"][0], s["pair"][1])
        for s in streams:
            s["inv"] = bdot(s["inv"], s["pair"][2])
        for s in streams:
            p4 = s["p4"]
            masked = jnp.concatenate([jnp.where(s["strict"], p4[:CHUNK, CHUNK:], 0.0),
                                      jnp.where(s["incl"], p4[CHUNK:, CHUNK:], 0.0)], axis=0)
            s["akv_rkv"] = bdot(masked, s["v_h"])
        for s in streams:
            s["m12"] = bdot(s["inv"], jnp.concatenate([s["kk_t"], s["akv_rkv"][:CHUNK]], axis=1)).astype(BF16)
        for s in streams:
            s["gh"] = _dot_tn(s["m12"], s["b_hat"])
            s["vk"] = _dot_tn(s["v_h"], s["k_hat"])
        for s in streams:
            s["qy"] = bdot(jnp.where(s["incl"], s["p4"][CHUNK:, :CHUNK], 0.0), s["m12"])
        for s in streams:
            g_mat = (ident - s["gh"][:HEAD_DIM]) * s["p_end"]
            h_mat = (s["vk"] - s["gh"][HEAD_DIM:]) * s["p_end"]
            q_mat = s["r_t"] - s["qy"][:, :HEAD_DIM]
            y0 = s["akv_rkv"][CHUNK:] - s["qy"][:, HEAD_DIM:]
            s0 = st_s[s["j"]].astype(BF16)
            y_s[s["d"], s["rows"], s["hc"]] = _dot_nt(q_mat.astype(BF16), s0) + y0
            st_s[s["j"]] = bdot(s0, g_mat) + h_mat
        return carry

    lax.fori_loop(0, nch // RW_CHUNKS_PER_STEP, chunk_step, 0)

    def finish(c, carry):
        rows = pl.ds(pl.multiple_of(c * CHUNK, CHUNK), CHUNK)
        for h in range(N_HEADS):
            hc = slice(h * HEAD_DIM, (h + 1) * HEAD_DIM)
            tot = _head_norm(y_s[0, rows, hc] + y_s[1, rows, hc], RW_NORM_EPS) * ng_ref[:, hc] + nb_ref[:, hc]
            out_ref[0, rows, hc] = ((tot + bonus_s[rows, hc]) * gate_s[rows, hc]).astype(BF16)
        return carry

    lax.fori_loop(0, nch, finish, 0)


def _rwkv(rkv, small, conv_w, w0, w_up, a0, a_up, g_up, k_k, k_a, r_k, norm_g, norm_b):
    bsz, s, _ = rkv.shape
    vec = lambda t: t.astype(F32).reshape(1, BRANCH_W)
    f = lambda t: t.astype(F32)
    return pl.pallas_call(
        _rwkv_body,
        out_shape=jax.ShapeDtypeStruct((bsz, s, BRANCH_W), BF16),
        grid=(bsz,),
        in_specs=[pl.BlockSpec((1, s, W_RKV), lambda b: (b, 0, 0)),
                  pl.BlockSpec((1, s, W_SMALL), lambda b: (b, 0, 0)),
                  _resident((3, 3 * BRANCH_W)), _resident((2, BRANCH_W)), _resident((2, RW_RANK, BRANCH_W)),
                  _resident((2, BRANCH_W)), _resident((2, RW_RANK, BRANCH_W)), _resident((2 * RW_RANK, BRANCH_W)),
                  _resident((1, BRANCH_W)), _resident((1, BRANCH_W)), _resident((1, BRANCH_W)),
                  _resident((1, BRANCH_W)), _resident((1, BRANCH_W))],
        out_specs=pl.BlockSpec((1, s, BRANCH_W), lambda b: (b, 0, 0)),
        scratch_shapes=[pltpu.VMEM((2, s, BRANCH_W), F32), pltpu.VMEM((s, BRANCH_W), F32), pltpu.VMEM((s, BRANCH_W), F32),
                        pltpu.VMEM((N_STREAMS, HEAD_DIM, HEAD_DIM), F32)],
        compiler_params=_params(("parallel",)),
        name="rwkv7",
    )(rkv, small, f(conv_w), f(w0), w_up.astype(BF16), f(a0), a_up.astype(BF16), g_up.astype(BF16), vec(k_k), vec(k_a),
      vec(r_k), vec(norm_g), vec(norm_b))


def _trunk(x, p, w_in_perm):
    bsz, s, d = x.shape
    n = bsz * s
    seq = lambda t: t.reshape(bsz, s, t.shape[-1])
    flat = lambda t: t.reshape(n, t.shape[-1])
    xf = _ln_rows(x.reshape(n, d), p["ln0_g"], p["ln0_b"])
    for l in range(DEPTH):
        na, mlqkv, mlo, rkv, small, merge = _in_proj(xf, w_in_perm[l])
        o_a = _na_attention(seq(na), _na_bias_table(p["na_rpb"][l], s // GRID_W))
        o_b = _mlstm(seq(mlqkv), seq(mlo), seq(small), p["ml_conv"][l], p["ml_gate_b"][l], p["ml_norm_g"][l])
        o_c = _rwkv(seq(rkv), seq(small), p["rw_conv"][l], p["rw_w0"][l], p["rw_w_up"][l], p["rw_a0"][l], p["rw_a_up"][l],
                    p["rw_g_up"][l], p["rw_k_k"][l], p["rw_k_a"][l], p["rw_r_k"][l], p["rw_norm_g"][l], p["rw_norm_b"][l])
        xf = _mix_ln(xf, flat(o_a), flat(o_b), flat(o_c), merge, p["w_br_a"][l], p["w_br_b"][l], p["w_br_c"][l],
                     p["w_out"][l], p["ln1_g"][l], p["ln1_b"][l])
        xf = _moe_ln(xf, p["w_router"], p["router_bias"], p["moe_w1"][l], p["moe_w3"][l], p["moe_w2"][l],
                     p["ln2_g"][l], p["ln2_b"][l])
    return xf.reshape(bsz, s, d)


def kernel(x_prompt, x_sample, ln0_g, ln0_b, w_in, na_rpb, ml_conv, ml_gate_b, ml_norm_g, rw_conv, rw_w0, rw_w_up,
           rw_a0, rw_a_up, rw_g_up, rw_k_k, rw_k_a, rw_r_k, rw_norm_g, rw_norm_b, w_br_a, w_br_b, w_br_c, w_out,
           ln1_g, ln1_b, w_router, router_bias, moe_w1, moe_w3, moe_w2, ln2_g, ln2_b):
    p = {
        "ln0_g": ln0_g, "ln0_b": ln0_b, "na_rpb": na_rpb, "ml_conv": ml_conv,
        "ml_gate_b": ml_gate_b, "ml_norm_g": ml_norm_g, "rw_conv": rw_conv, "rw_w0": rw_w0,
        "rw_w_up": rw_w_up, "rw_a0": rw_a0, "rw_a_up": rw_a_up, "rw_g_up": rw_g_up, "rw_k_k": rw_k_k,
        "rw_k_a": rw_k_a, "rw_r_k": rw_r_k, "rw_norm_g": rw_norm_g, "rw_norm_b": rw_norm_b,
        "w_br_a": w_br_a, "w_br_b": w_br_b, "w_br_c": w_br_c, "w_out": w_out, "ln1_g": ln1_g,
        "ln1_b": ln1_b, "w_router": w_router, "router_bias": router_bias, "moe_w1": moe_w1,
        "moe_w3": moe_w3, "moe_w2": moe_w2, "ln2_g": ln2_g, "ln2_b": ln2_b,
    }
    w_in_perm = [_permute_w_in(w_in[l]) for l in range(DEPTH)]
    return _trunk(x_prompt, p, w_in_perm), _trunk(x_sample, p, w_in_perm)
```

```python
import functools

import jax
import jax.numpy as jnp
import numpy as np
from jax import lax
from jax.experimental import pallas as pl
from jax.experimental.pallas import tpu as pltpu

F32 = jnp.float32
BF16 = jnp.bfloat16

D_MODEL = 1024
DEPTH = 4
GRID_W = 64
HEAD_DIM = 64
N_HEADS = 4
BRANCH_W = N_HEADS * HEAD_DIM
CHUNK = 64
NA_WIN_ROWS = 8
NA_WIN_COLS = 16
NA_ROW_CLASSES = 8
ML_NORM_EPS = 1e-6
RW_RANK = 32
RW_DECAY_SCALE = 0.606531
RW_NORM_EPS = 64e-5
N_EXPERTS = 16
EXPERTS_PER_GROUP = 4
D_EXPERT = 512
ALPHA = (2 * DEPTH) ** 0.25
LN_EPS = 1e-5
NEG = -1e30

W_NA = 3 * BRANCH_W
W_MLQKV = 3 * BRANCH_W
W_MLO = BRANCH_W
W_RKV = 3 * BRANCH_W
W_SMALL = 256
W_MERGE = 3 * D_MODEL
PIECES = (W_NA, W_MLQKV, W_MLO, W_RKV, W_SMALL, W_MERGE)
D_INP = sum(PIECES)
SMALL_GATE_OFF = 192

VMEM_LIMIT = 56 * 1024 * 1024
TOKEN_TILE = 512


def _params(sem):
    return pltpu.CompilerParams(dimension_semantics=sem, vmem_limit_bytes=VMEM_LIMIT)


def _resident(shape):
    nd = len(shape)
    return pl.BlockSpec(shape, lambda *_: (0,) * nd, pipeline_mode=pl.Buffered(1))


def _ln(z, g, b):
    mu = jnp.mean(z, axis=-1, keepdims=True)
    zc = z - mu
    var = jnp.mean(zc * zc, axis=-1, keepdims=True)
    return zc * lax.rsqrt(var + LN_EPS) * g + b


def _ln_rows_body(x_ref, g_ref, b_ref, o_ref):
    o_ref[...] = _ln(x_ref[...], g_ref[...], b_ref[...])


def _ln_rows(x, g, b):
    n, d = x.shape
    tm = TOKEN_TILE
    return pl.pallas_call(
        _ln_rows_body,
        out_shape=jax.ShapeDtypeStruct((n, d), F32),
        grid=(n // tm,),
        in_specs=[pl.BlockSpec((tm, d), lambda i: (i, 0)), _resident((1, d)), _resident((1, d))],
        out_specs=pl.BlockSpec((tm, d), lambda i: (i, 0)),
        compiler_params=_params(("parallel",)),
        name="ln_rows",
    )(x, g.reshape(1, d), b.reshape(1, d))


def _in_proj_body(x_ref, w_ref, *o_refs):
    xb = x_ref[...].astype(BF16)
    off = 0
    for o_ref, width in zip(o_refs, PIECES):
        for c0 in range(0, width, 768):
            c1 = min(c0 + 768, width)
            o_ref[:, c0:c1] = jnp.dot(xb, w_ref[:, off + c0:off + c1], preferred_element_type=F32).astype(BF16)
        off += width


def _in_proj(x, w):
    n, d = x.shape
    tm = TOKEN_TILE
    return pl.pallas_call(
        _in_proj_body,
        out_shape=[jax.ShapeDtypeStruct((n, wd), BF16) for wd in PIECES],
        grid=(n // tm,),
        in_specs=[pl.BlockSpec((tm, d), lambda i: (i, 0)), _resident((d, D_INP))],
        out_specs=[pl.BlockSpec((tm, wd), lambda i: (i, 0)) for wd in PIECES],
        compiler_params=_params(("parallel",)),
        name="in_proj",
    )(x, w)


def _permute_w_in(w_in_l):
    na, mlqkv, mlo, mlg, rkv, wl, al, gl, merge = jnp.split(
        w_in_l, np.cumsum([768, 768, 256, 16, 768, 64, 64, 64])[:8].tolist(), axis=1)
    pad = jnp.zeros((w_in_l.shape[0], W_SMALL - 208), w_in_l.dtype)
    return jnp.concatenate([na, mlqkv, mlo, rkv, wl, al, gl, mlg, pad, merge], axis=1).astype(BF16)


def _mix_ln_body(x_ref, oa_ref, ob_ref, oc_ref, mg_ref, wa_ref, wb_ref, wc_ref, wo_ref, g_ref, b_ref, o_ref):
    mixed = None
    for j, (o_br, w_br) in enumerate(((oa_ref, wa_ref), (ob_ref, wb_ref), (oc_ref, wc_ref))):
        gate = jax.nn.sigmoid(mg_ref[:, j * D_MODEL:(j + 1) * D_MODEL].astype(F32))
        term = gate * jnp.dot(o_br[...], w_br[...], preferred_element_type=F32)
        mixed = term if mixed is None else mixed + term
    z = ALPHA * x_ref[...] + jnp.dot(mixed.astype(BF16), wo_ref[...], preferred_element_type=F32)
    o_ref[...] = _ln(z, g_ref[...], b_ref[...])


def _mix_ln(x, oa, ob, oc, merge, wa, wb, wc, wo, g, b):
    n, d = x.shape
    tm = TOKEN_TILE
    row = lambda wd: pl.BlockSpec((tm, wd), lambda i: (i, 0))
    return pl.pallas_call(
        _mix_ln_body,
        out_shape=jax.ShapeDtypeStruct((n, d), F32),
        grid=(n // tm,),
        in_specs=[row(d), row(BRANCH_W), row(BRANCH_W), row(BRANCH_W), row(W_MERGE),
                  _resident((BRANCH_W, d)), _resident((BRANCH_W, d)), _resident((BRANCH_W, d)), _resident((d, d)),
                  _resident((1, d)), _resident((1, d))],
        out_specs=row(d),
        compiler_params=_params(("parallel",)),
        name="mix_ln",
    )(x, oa, ob, oc, merge, wa.astype(BF16), wb.astype(BF16), wc.astype(BF16), wo.astype(BF16),
      g.reshape(1, d), b.reshape(1, d))


MOE_TILE = 1024


def _top2_sum(a, b, c, d):
    hi1, lo1 = jnp.maximum(a, b), jnp.minimum(a, b)
    hi2, lo2 = jnp.maximum(c, d), jnp.minimum(c, d)
    return jnp.maximum(hi1, hi2) + jnp.maximum(jnp.minimum(hi1, hi2), jnp.maximum(lo1, lo2))


def _router_gates(x, wrt, rbias):
    logits = lax.dot_general(wrt, x, (((1,), (1,)), ((), ())), precision=lax.Precision.HIGHEST,
                             preferred_element_type=F32)
    ex = jnp.exp(logits - jnp.max(logits, axis=0, keepdims=True))
    probs = ex / jnp.sum(ex, axis=0, keepdims=True)
    sel = probs + rbias
    rows = [sel[e:e + 1, :] for e in range(N_EXPERTS)]
    n_groups = N_EXPERTS // EXPERTS_PER_GROUP
    scores = [_top2_sum(*rows[EXPERTS_PER_GROUP * g:EXPERTS_PER_GROUP * (g + 1)]) for g in range(n_groups)]
    best, gidx = scores[0], jnp.zeros_like(scores[0], dtype=jnp.int32)
    for g in range(1, n_groups):
        better = scores[g] > best
        gidx = jnp.where(better, g, gidx)
        best = jnp.where(better, scores[g], best)
    picked = []
    for e in range(N_EXPERTS):
        g = e // EXPERTS_PER_GROUP
        rank = jnp.zeros_like(gidx)
        for e2 in range(EXPERTS_PER_GROUP * g, EXPERTS_PER_GROUP * (g + 1)):
            if e2 == e:
                continue
            ahead = (rows[e2] >= rows[e]) if e2 < e else (rows[e2] > rows[e])
            rank = rank + ahead.astype(jnp.int32)
        chosen = (gidx == g) & (rank < 2)
        picked.append(jnp.where(chosen, probs[e:e + 1, :], 0.0))
    total = picked[0]
    for p in picked[1:]:
        total = total + p
    slots = []
    for k in range(EXPERTS_PER_GROUP):
        acc = picked[k]
        for g in range(1, n_groups):
            acc = acc + picked[EXPERTS_PER_GROUP * g + k]
        slots.append(acc / total)
    return jnp.concatenate(slots, axis=0), gidx


def _moe_ln_body(x_ref, wrt_ref, rb_ref, tri_ref, w1_ref, w3_ref, w2_ref, g_ref, b_ref, o_ref,
                 xb_s, rank_s, gidx_s, slot_s, xs_s, ys_s, rs_s, gs_s, acc_s, cnt_s):
    e = pl.program_id(1)
    grp = e // EXPERTS_PER_GROUP
    k = e % EXPERTS_PER_GROUP
    tm = x_ref.shape[0]
    n_groups = N_EXPERTS // EXPERTS_PER_GROUP

    @pl.when(e == 0)
    def _():
        x = x_ref[...]
        xb_s[...] = x.astype(BF16)
        slots, gidx = _router_gates(x, wrt_ref[...], rb_ref[...])
        hi = slots.astype(BF16)
        slot_s[...] = jnp.concatenate([hi, (slots - hi.astype(F32)).astype(BF16)], axis=0)
        gidx_s[...] = gidx
        member = [jnp.where(gidx == g, 1.0, 0.0) for g in range(n_groups)]
        onehot = jnp.concatenate(member + [jnp.zeros((8 - n_groups, tm), F32)], axis=0)
        upto = jnp.dot(onehot.astype(BF16), tri_ref[...], preferred_element_type=F32)
        rank = -1.0
        for g in range(n_groups):
            rank = rank + member[g] * upto[g:g + 1, :]
            cnt_s[g] = jnp.sum(member[g]).astype(jnp.int32)
        rank_s[...] = rank
        acc_s[...] = jnp.zeros_like(acc_s)

    n_blocks = (cnt_s[grp] + MOE_BLOCK - 1) // MOE_BLOCK

    @pl.when(k == 0)
    def _():
        def compact(blk, carry):
            want = (lax.broadcasted_iota(jnp.int32, (MOE_BLOCK, tm), 0) + blk * MOE_BLOCK).astype(F32)
            pick = jnp.where((rank_s[...] == want) & (gidx_s[...] == grp), 1.0, 0.0).astype(BF16)
            rs_s[blk] = pick
            xs_s[blk] = jnp.dot(pick, xb_s[...], preferred_element_type=F32).astype(BF16)
            gs_s[blk] = _dot_nt(pick, slot_s[...])
            return carry

        lax.fori_loop(0, n_blocks, compact, 0)

    lane = lax.broadcasted_iota(jnp.int32, (MOE_BLOCK, 2 * EXPERTS_PER_GROUP), 1)

    def expert_block(blk, first):
        xs = xs_s[blk]
        h1 = jnp.dot(xs, w1_ref[0], preferred_element_type=F32)
        h3 = jnp.dot(xs, w3_ref[0], preferred_element_type=F32)
        y = jnp.dot((h1 * jax.nn.sigmoid(h1) * h3).astype(BF16), w2_ref[0], preferred_element_type=F32)
        gate = jnp.sum(jnp.where(lane % EXPERTS_PER_GROUP == k, gs_s[blk], 0.0), axis=1, keepdims=True)
        if first:
            ys_s[blk] = gate * y
        else:
            ys_s[blk] += gate * y

    @pl.when(k == 0)
    def _():
        lax.fori_loop(0, n_blocks, lambda blk, c: (expert_block(blk, True), c)[1], 0)

    @pl.when(k > 0)
    def _():
        lax.fori_loop(0, n_blocks, lambda blk, c: (expert_block(blk, False), c)[1], 0)

    @pl.when(k == EXPERTS_PER_GROUP - 1)
    def _():
        def scatter(blk, carry):
            acc_s[...] += _dot_tn(rs_s[blk], ys_s[blk].astype(BF16))
            return carry

        lax.fori_loop(0, n_blocks, scatter, 0)

    @pl.when(e == N_EXPERTS - 1)
    def _():
        o_ref[...] = _ln(ALPHA * x_ref[...] + acc_s[...], g_ref[...], b_ref[...])


MOE_BLOCK = 320


def _moe_ln(x, w_router, router_bias, w1, w3, w2, g, b):
    n, d = x.shape
    tm = MOE_TILE
    nb = pl.cdiv(tm, MOE_BLOCK)
    tri = (np.arange(tm)[:, None] <= np.arange(tm)[None, :])
    return pl.pallas_call(
        _moe_ln_body,
        out_shape=jax.ShapeDtypeStruct((n, d), F32),
        grid=(n // tm, N_EXPERTS),
        in_specs=[pl.BlockSpec((tm, d), lambda i, e: (i, 0)),
                  _resident((N_EXPERTS, d)), _resident((N_EXPERTS, 1)), _resident((tm, tm)),
                  pl.BlockSpec((1, d, D_EXPERT), lambda i, e: (e, 0, 0)),
                  pl.BlockSpec((1, d, D_EXPERT), lambda i, e: (e, 0, 0)),
                  pl.BlockSpec((1, D_EXPERT, d), lambda i, e: (e, 0, 0)),
                  _resident((1, d)), _resident((1, d))],
        out_specs=pl.BlockSpec((tm, d), lambda i, e: (i, 0)),
        scratch_shapes=[pltpu.VMEM((tm, d), BF16), pltpu.VMEM((1, tm), F32), pltpu.VMEM((1, tm), jnp.int32),
                        pltpu.VMEM((2 * EXPERTS_PER_GROUP, tm), BF16),
                        pltpu.VMEM((nb, MOE_BLOCK, d), BF16), pltpu.VMEM((nb, MOE_BLOCK, d), F32),
                        pltpu.VMEM((nb, MOE_BLOCK, tm), BF16), pltpu.VMEM((nb, MOE_BLOCK, 2 * EXPERTS_PER_GROUP), F32),
                        pltpu.VMEM((tm, d), F32), pltpu.SMEM((N_EXPERTS // EXPERTS_PER_GROUP,), jnp.int32)],
        compiler_params=_params(("parallel", "arbitrary")),
        name="moe_ln",
    )(x, w_router.T, router_bias.reshape(N_EXPERTS, 1), jnp.asarray(tri, BF16), w1.astype(BF16), w3.astype(BF16),
      w2.astype(BF16), g.reshape(1, d), b.reshape(1, d))


def _na_bias_table(rpb, rows):
    assert rows >= NA_WIN_ROWS
    cols = np.arange(GRID_W)
    win = np.clip(cols - NA_WIN_COLS // 2, 0, GRID_W - NA_WIN_COLS)
    rel = cols[None, :] - cols[:, None]
    ok = (cols[None, :] >= win[:, None]) & (cols[None, :] < win[:, None] + NA_WIN_COLS)
    pick = (np.arange(2 * NA_WIN_COLS - 1)[None, None, :] == (rel + NA_WIN_COLS - 1)[:, :, None]) & ok[:, :, None]
    t = jnp.einsum("hdr,ckr->hdck", rpb.astype(F32), jnp.asarray(pick, F32), precision=lax.Precision.HIGHEST)
    t = jnp.where(ok[None, None], t, NEG)
    per_class = [t[:, NA_WIN_ROWS - 1 - cls:2 * NA_WIN_ROWS - 1 - cls] for cls in range(NA_ROW_CLASSES)]
    t = jnp.stack(per_class, axis=1)
    return jnp.transpose(t, (0, 1, 3, 2, 4)).reshape(rpb.shape[0], NA_ROW_CLASSES, GRID_W, NA_WIN_ROWS * GRID_W)


NA_ROWS_PER_STEP = 2


def _na_body(qkv_ref, bias_ref, o_ref):
    rows = qkv_ref.shape[1] // GRID_W
    band = NA_WIN_ROWS * GRID_W

    head_cols = lambda base, h: slice(base + h * HEAD_DIM, base + (h + 1) * HEAD_DIM)

    def row_step(i, carry):
        units = []
        for u in range(NA_ROWS_PER_STEP):
            r = NA_ROWS_PER_STEP * i + u
            rs = jnp.clip(r - NA_WIN_ROWS // 2, 0, rows - NA_WIN_ROWS)
            cls = jnp.where(r < NA_WIN_ROWS // 2, r, NA_WIN_ROWS // 2 + jnp.maximum(r - (rows - NA_WIN_ROWS // 2), 0))
            q_rows = pl.ds(pl.multiple_of(r * GRID_W, GRID_W), GRID_W)
            k_rows = pl.ds(pl.multiple_of(rs * GRID_W, GRID_W), band)
            units += [(h, cls, q_rows, k_rows) for h in range(N_HEADS)]
        scores = [_dot_nt(qkv_ref[0, q_rows, head_cols(0, h)], qkv_ref[0, k_rows, head_cols(BRANCH_W, h)])
                  for h, _, q_rows, k_rows in units]
        maxes = [jnp.max(scores[n] * (HEAD_DIM ** -0.5) + bias_ref[h, cls], axis=1, keepdims=True)
                 for n, (h, cls, _, _) in enumerate(units)]
        probs = [jnp.exp(scores[n] * (HEAD_DIM ** -0.5) + bias_ref[h, cls] - maxes[n]) for n, (h, cls, _, _) in enumerate(units)]
        sums = [jnp.sum(p, axis=1, keepdims=True) for p in probs]
        outs = [jnp.dot(probs[n].astype(BF16), qkv_ref[0, k_rows, head_cols(2 * BRANCH_W, h)], preferred_element_type=F32)
                for n, (h, _, _, k_rows) in enumerate(units)]
        for n, (h, _, q_rows, _) in enumerate(units):
            o_ref[0, q_rows, head_cols(0, h)] = (outs[n] / sums[n]).astype(BF16)
        return carry

    lax.fori_loop(0, rows // NA_ROWS_PER_STEP, row_step, 0)


def _na_attention(qkv, bias):
    bsz, s, _ = qkv.shape
    return pl.pallas_call(
        _na_body,
        out_shape=jax.ShapeDtypeStruct((bsz, s, BRANCH_W), BF16),
        grid=(bsz,),
        in_specs=[pl.BlockSpec((1, s, W_NA), lambda b: (b, 0, 0)), _resident(bias.shape)],
        out_specs=pl.BlockSpec((1, s, BRANCH_W), lambda b: (b, 0, 0)),
        compiler_params=_params(("parallel",)),
        name="na_attention",
    )(qkv, bias)


def _eye(n):
    return lax.broadcasted_iota(jnp.int32, (n, n), 0) == lax.broadcasted_iota(jnp.int32, (n, n), 1)


def _row_to_col(row, eye):
    return jnp.sum(jnp.where(eye, row, 0.0), axis=1, keepdims=True)


def _conv3_rows(ref, c, nch, cols, w_ref):
    seq = nch * CHUNK
    r0 = pl.multiple_of(c * CHUNK, CHUNK)
    x = ref[0, pl.ds(r0, CHUNK), cols].astype(F32)
    lo = pl.multiple_of(jnp.maximum(r0 - 16, 0), 16)
    hi = pl.multiple_of(jnp.minimum(r0 + CHUNK, seq - 16), 16)
    prev = ref[0, pl.ds(lo, 16), cols][15:16, :].astype(F32) * jnp.where(c > 0, 1.0, 0.0)
    nxt = ref[0, pl.ds(hi, 16), cols][0:1, :].astype(F32) * jnp.where(c < nch - 1, 1.0, 0.0)
    rid = lax.broadcasted_iota(jnp.int32, x.shape, 0)
    xm = jnp.where(rid == 0, prev, pltpu.roll(x, 1, axis=0))
    xp = jnp.where(rid == CHUNK - 1, nxt, pltpu.roll(x, CHUNK - 1, axis=0))
    return w_ref[0:1, :] * xm + w_ref[1:2, :] * x + w_ref[2:3, :] * xp


def _head_norm(h, eps):
    mu = jnp.mean(h, axis=-1, keepdims=True)
    hc = h - mu
    return hc * lax.rsqrt(jnp.mean(hc * hc, axis=-1, keepdims=True) + eps)


def _dot_nt(a, b, **kw):
    return lax.dot_general(a, b, (((1,), (1,)), ((), ())), preferred_element_type=F32, **kw)


def _dot_tn(a, b, **kw):
    return lax.dot_general(a, b, (((0,), (0,)), ((), ())), preferred_element_type=F32, **kw)


N_STREAMS = 2 * N_HEADS


STATE_W = 2 * HEAD_DIM


def _mlstm_body(qkv_ref, o_ref_in, g_ref, gb_ref, cw_ref, ng_ref, out_ref,
                qk_s, vx_s, ig_s, b_s, ct_s, mc_s, bl_s, mprev_s, cst_s, mst_s):
    seq = qkv_ref.shape[1]
    nch = seq // CHUNK
    eye = _eye(CHUNK)
    ri = lax.broadcasted_iota(jnp.int32, (CHUNK, CHUNK), 0)
    ci = lax.broadcasted_iota(jnp.int32, (CHUNK, CHUNK), 1)
    head_cols = lambda base, h: slice(base + h * HEAD_DIM, base + (h + 1) * HEAD_DIM)
    ones_col = jnp.where(lax.broadcasted_iota(jnp.int32, (CHUNK, HEAD_DIM), 1) == 0, 1.0, 0.0).astype(BF16)

    def conv_step(c, carry):
        rows = pl.ds(pl.multiple_of(c * CHUNK, CHUNK), CHUNK)
        y = _conv3_rows(qkv_ref, c, nch, slice(0, 2 * BRANCH_W), cw_ref)
        y = y * jax.nn.sigmoid(y)
        scale = jnp.where(lax.broadcasted_iota(jnp.int32, (1, 2 * BRANCH_W), 1) < BRANCH_W, 1.0, HEAD_DIM ** -0.5)
        qk_s[rows, :] = (y * scale).astype(BF16)
        for h in range(N_HEADS):
            vx_s[rows, h * STATE_W:h * STATE_W + HEAD_DIM] = qkv_ref[0, rows, head_cols(2 * BRANCH_W, h)]
            vx_s[rows, h * STATE_W + HEAD_DIM:(h + 1) * STATE_W] = ones_col
        return carry

    lax.fori_loop(0, nch, conv_step, 0)

    for j in range(N_STREAMS):
        rev, h = j >= N_HEADS, j % N_HEADS
        gi = (2 * N_HEADS if rev else 0) + h
        ig_s[j] = g_ref[0, gi] + gb_ref[gi]
        fpre = g_ref[0, gi + N_HEADS] + gb_ref[gi + N_HEADS]
        lf = jnp.minimum(fpre, 0.0) - jnp.log1p(jnp.exp(-jnp.abs(fpre)))
        cum = (ri >= ci) if rev else (ri <= ci)
        b_s[j] = jnp.dot(lf, cum.astype(F32), precision=lax.Precision.HIGHEST, preferred_element_type=F32)
        cst_s[j] = jnp.zeros((HEAD_DIM, STATE_W), F32)
        mst_s[j] = jnp.full((1, 128), NEG, F32)

    def chunk_stats(c, carry):
        rows = pl.ds(pl.multiple_of(c * CHUNK, CHUNK), CHUNK)
        a_rs, mcs = [], []
        for j in range(N_STREAMS):
            b_r = b_s[j, pl.ds(c, 1), :]
            bl = b_r[:, 0:1] if j >= N_HEADS else b_r[:, CHUNK - 1:CHUNK]
            a_rs.append(bl - b_r + ig_s[j, pl.ds(c, 1), :])
            bl_s[j, c] = jnp.broadcast_to(bl, (1, 128))
        for j in range(N_STREAMS):
            mcs.append(jnp.max(a_rs[j], axis=1, keepdims=True))
            mc_s[j, c] = jnp.broadcast_to(mcs[j], (1, 128))
        wa_cs = [_row_to_col(jnp.exp(a_rs[j] - mcs[j]), eye) for j in range(N_STREAMS)]
        kws = [(qk_s[rows, head_cols(BRANCH_W, j % N_HEADS)].astype(F32) * wa_cs[j]).astype(BF16) for j in range(N_STREAMS)]
        for j in range(N_STREAMS):
            h = j % N_HEADS
            ct_s[j, c] = _dot_tn(kws[j], vx_s[rows, h * STATE_W:(h + 1) * STATE_W])
        return carry

    lax.fori_loop(0, nch, chunk_stats, 0)

    def scan_step(i, carry):
        for j in range(N_STREAMS):
            c = (nch - 1 - i) if j >= N_HEADS else i
            ct, m = cst_s[j], mst_s[j]
            bl, mc = bl_s[j, c], mc_s[j, c]
            m_new = jnp.maximum(bl + m, mc)
            cst_s[j] = jnp.exp(bl + m - m_new) * ct + jnp.exp(mc - m_new) * ct_s[j, c]
            mst_s[j] = m_new
            ct_s[j, c] = ct
            mprev_s[j, c] = m
        return carry

    lax.fori_loop(0, nch, scan_step, 0)

    def chunk_out(c, carry):
        rows = pl.ds(pl.multiple_of(c * CHUNK, CHUNK), CHUNK)
        qs = [qk_s[rows, head_cols(0, h)] for h in range(N_HEADS)]
        vs = [vx_s[rows, h * STATE_W:(h + 1) * STATE_W] for h in range(N_HEADS)]
        qks = [_dot_nt(qs[h], qk_s[rows, head_cols(BRANCH_W, h)]) for h in range(N_HEADS)]
        inter_parts = [jnp.dot(qs[j % N_HEADS], ct_s[j, c].astype(BF16), preferred_element_type=F32) for j in range(N_STREAMS)]
        b_rs = [b_s[j, pl.ds(c, 1), :] for j in range(N_STREAMS)]
        b_cs = [_row_to_col(b_rs[j], eye) for j in range(N_STREAMS)]
        dmats = []
        for j in range(N_STREAMS):
            tri = (ci >= ri) if j >= N_HEADS else (ci <= ri)
            dmats.append(jnp.where(tri, b_cs[j] - b_rs[j] + ig_s[j, pl.ds(c, 1), :], NEG))
        m_intras = [jnp.max(dmats[j], axis=1, keepdims=True) for j in range(N_STREAMS)]
        stats = []
        for j in range(N_STREAMS):
            g_c = b_cs[j] + mprev_s[j, c][:, 0:1]
            m_t = jnp.maximum(g_c, m_intras[j])
            stats.append((jnp.exp(g_c - m_t), qks[j % N_HEADS] * jnp.exp(dmats[j] - m_t), m_t))
        intra_parts = [jnp.dot(stats[j][1].astype(BF16), vs[j % N_HEADS], preferred_element_type=F32) for j in range(N_STREAMS)]
        hs = []
        for j in range(N_STREAMS):
            inter, _, m_t = stats[j]
            both = inter * inter_parts[j] + intra_parts[j]
            den = both[:, HEAD_DIM:HEAD_DIM + 1]
            hs.append(both[:, :HEAD_DIM] / jnp.maximum(jnp.abs(den), jnp.exp(-m_t)))
        for h in range(N_HEADS):
            hc = head_cols(0, h)
            tot = _head_norm(hs[h] + hs[N_HEADS + h], ML_NORM_EPS) * ng_ref[:, hc]
            out_ref[0, rows, hc] = (tot * jax.nn.sigmoid(o_ref_in[0, rows, hc].astype(F32))).astype(BF16)
        return carry

    lax.fori_loop(0, nch, chunk_out, 0)


def _mlstm(qkv, o_pre, small, conv_w, gate_b, norm_g):
    bsz, s, _ = qkv.shape
    nch = s // CHUNK
    n_gate = 4 * N_HEADS
    gates = small[..., SMALL_GATE_OFF:SMALL_GATE_OFF + n_gate].astype(F32)
    gates = jnp.transpose(gates, (0, 2, 1)).reshape(bsz, n_gate, nch, CHUNK)
    gate_b = jnp.broadcast_to(gate_b.astype(F32).reshape(n_gate, 1, 1), (n_gate, 1, CHUNK))
    row = lambda shape: pltpu.VMEM(shape, F32)
    return pl.pallas_call(
        _mlstm_body,
        out_shape=jax.ShapeDtypeStruct((bsz, s, BRANCH_W), BF16),
        grid=(bsz,),
        in_specs=[pl.BlockSpec((1, s, W_MLQKV), lambda b: (b, 0, 0)),
                  pl.BlockSpec((1, s, BRANCH_W), lambda b: (b, 0, 0)),
                  pl.BlockSpec((1, n_gate, nch, CHUNK), lambda b: (b, 0, 0, 0)),
                  _resident((n_gate, 1, CHUNK)), _resident((3, 2 * BRANCH_W)), _resident((1, BRANCH_W))],
        out_specs=pl.BlockSpec((1, s, BRANCH_W), lambda b: (b, 0, 0)),
        scratch_shapes=[pltpu.VMEM((s, 2 * BRANCH_W), BF16), pltpu.VMEM((s, N_HEADS * STATE_W), BF16),
                        row((N_STREAMS, nch, CHUNK)), row((N_STREAMS, nch, CHUNK)),
                        row((N_STREAMS, nch, HEAD_DIM, STATE_W)),
                        row((N_STREAMS, nch, 1, 128)), row((N_STREAMS, nch, 1, 128)), row((N_STREAMS, nch, 1, 128)),
                        row((N_STREAMS, HEAD_DIM, STATE_W)), row((N_STREAMS, 1, 128))],
        compiler_params=_params(("parallel",)),
        name="mlstm",
    )(qkv, o_pre, gates, gate_b, conv_w.astype(F32), norm_g.astype(F32).reshape(1, BRANCH_W))


RW_CHUNKS_PER_STEP = 2


def _rwkv_body(rkv_ref, sm_ref, cw_ref, w0_ref, wup_ref, a0_ref, aup_ref, gup_ref, kk_ref, ka_ref, rk_ref, ng_ref, nb_ref,
               out_ref, y_s, bonus_s, gate_s, st_s):
    seq = rkv_ref.shape[1]
    nch = seq // CHUNK
    ri = lax.broadcasted_iota(jnp.int32, (CHUNK, CHUNK), 0)
    ci = lax.broadcasted_iota(jnp.int32, (CHUNK, CHUNK), 1)
    ident = (ri == ci).astype(F32)

    bdot = lambda x, y: jnp.dot(x.astype(BF16), y.astype(BF16), preferred_element_type=F32)

    def low_rank(x, d, up_ref, bias_ref):
        return bias_ref[d:d + 1, :] + bdot(x[:, d * RW_RANK:(d + 1) * RW_RANK], up_ref[d])

    st_s[...] = jnp.zeros_like(st_s)

    def chunk_step(i, carry):
        streams = []
        for rev, u in ((False, 0), (False, 1), (True, 0), (True, 1)):
            d = 1 if rev else 0
            strict = (ci > ri) if rev else (ci < ri)
            incl = (ci >= ri) if rev else (ci <= ri)
            last = slice(0, 1) if rev else slice(CHUNK - 1, CHUNK)
            c = (nch - 1 - (RW_CHUNKS_PER_STEP * i + u)) if rev else (RW_CHUNKS_PER_STEP * i + u)
            rows = pl.ds(pl.multiple_of(c * CHUNK, CHUNK), CHUNK)
            rkv = _conv3_rows(rkv_ref, c, nch, slice(0, 3 * BRANCH_W), cw_ref)
            r, k, v = rkv[:, :BRANCH_W], rkv[:, BRANCH_W:2 * BRANCH_W], rkv[:, 2 * BRANCH_W:]
            sm = sm_ref[0, rows, :].astype(F32)
            w_lo, a_lo, g_lo = jnp.tanh(sm[:, :2 * RW_RANK]), sm[:, 2 * RW_RANK:4 * RW_RANK], sm[:, 4 * RW_RANK:6 * RW_RANK]
            lw = -RW_DECAY_SCALE * jax.nn.sigmoid(low_rank(w_lo, d, wup_ref, w0_ref))
            a = jax.nn.sigmoid(low_rank(a_lo, d, aup_ref, a0_ref))
            kd = k * (1.0 + (a - 1.0) * ka_ref[...])
            kk_raw = k * kk_ref[...]
            lw_hi = lw.astype(BF16)
            lw_lo = (lw - lw_hi.astype(F32)).astype(BF16)
            cum = jnp.dot(incl.astype(BF16), jnp.concatenate([lw_hi, lw_lo], axis=1), preferred_element_type=F32)
            lp_inc = cum[:, :BRANCH_W] + cum[:, BRANCH_W:]
            p_end = jnp.exp(lp_inc[last, :])
            e_inc = jnp.exp(lp_inc)
            e_inv = jnp.exp(-lp_inc)
            e_exc = jnp.exp(lp_inc - lw)
            if not rev:
                a_b = jax.nn.sigmoid(low_rank(a_lo, 1, aup_ref, a0_ref))
                kd_sum = kd + k * (1.0 + (a_b - 1.0) * ka_ref[...])
                gate_s[rows, :] = bdot(jax.nn.sigmoid(g_lo), gup_ref[...])
            for h in range(N_HEADS):
                hc = slice(h * HEAD_DIM, (h + 1) * HEAD_DIM)
                kk_h = kk_raw[:, hc]
                kk_h = kk_h / jnp.maximum(jnp.sqrt(jnp.sum(kk_h * kk_h, axis=1, keepdims=True)), 1e-12)
                streams.append(dict(
                    j=d * N_HEADS + h, d=d, rows=rows, hc=hc, strict=strict, incl=incl, p_end=p_end[:, hc],
                    b_hat=(kk_h * a[:, hc] * e_inv[:, hc]).astype(BF16), k_hat=(kd[:, hc] * e_inv[:, hc]).astype(BF16),
                    kk_t=kk_h * e_exc[:, hc], r_t=r[:, hc] * e_inc[:, hc], v_h=v[:, hc].astype(BF16)))
                if not rev:
                    bonus_s[rows, hc] = jnp.sum(r[:, hc] * kd_sum[:, hc] * rk_ref[:, hc], axis=1, keepdims=True) * v[:, hc]

        for s in streams:
            s["p4"] = _dot_nt(jnp.concatenate([s["kk_t"], s["r_t"]], axis=0).astype(BF16),
                              jnp.concatenate([s["b_hat"], s["k_hat"]], axis=0))
            s["pw"] = [jnp.where(s["strict"], -s["p4"][:CHUNK, :CHUNK], 0.0).astype(BF16)]
            s["pair"] = []
        for step in range(1, 6):
            for s in streams:
                s["pw"].append(jnp.dot(s["pw"][-1], s["pw"][-1], preferred_element_type=F32).astype(BF16))
            if step % 2 == 1:
                for s in streams:
                    lo, hi = s["pw"][step - 1], s["pw"][step]
                    s["pair"].append(ident + lo.astype(F32) + hi.astype(F32) + jnp.dot(lo, hi, preferred_element_type=F32))
        for s in streams:
            s["inv"] = bdot(s["pair"][0], s["pair"][1])
        for s in streams:
            s["inv"] = bdot(s["inv"], s["pair"][2])
        for s in streams:
            p4 = s["p4"]
            masked = jnp.concatenate([jnp.where(s["strict"], p4[:CHUNK, CHUNK:], 0.0),
                                      jnp.where(s["incl"], p4[CHUNK:, CHUNK:], 0.0)], axis=0)
            s["akv_rkv"] = bdot(masked, s["v_h"])
        for s in streams:
            s["m12"] = bdot(s["inv"], jnp.concatenate([s["kk_t"], s["akv_rkv"][:CHUNK]], axis=1)).astype(BF16)
        for s in streams:
            s["gh"] = _dot_tn(s["m12"], s["b_hat"])
            s["vk"] = _dot_tn(s["v_h"], s["k_hat"])
        for s in streams:
            s["qy"] = bdot(jnp.where(s["incl"], s["p4"][CHUNK:, :CHUNK], 0.0), s["m12"])
        for s in streams:
            g_mat = (ident - s["gh"][:HEAD_DIM]) * s["p_end"]
            h_mat = (s["vk"] - s["gh"][HEAD_DIM:]) * s["p_end"]
            q_mat = s["r_t"] - s["qy"][:, :HEAD_DIM]
            y0 = s["akv_rkv"][CHUNK:] - s["qy"][:, HEAD_DIM:]
            s0 = st_s[s["j"]].astype(BF16)
            y_s[s["d"], s["rows"], s["hc"]] = _dot_nt(q_mat.astype(BF16), s0) + y0
            st_s[s["j"]] = bdot(s0, g_mat) + h_mat
        return carry

    lax.fori_loop(0, nch // RW_CHUNKS_PER_STEP, chunk_step, 0)

    def finish(c, carry):
        rows = pl.ds(pl.multiple_of(c * CHUNK, CHUNK), CHUNK)
        for h in range(N_HEADS):
            hc = slice(h * HEAD_DIM, (h + 1) * HEAD_DIM)
            tot = _head_norm(y_s[0, rows, hc] + y_s[1, rows, hc], RW_NORM_EPS) * ng_ref[:, hc] + nb_ref[:, hc]
            out_ref[0, rows, hc] = ((tot + bonus_s[rows, hc]) * gate_s[rows, hc]).astype(BF16)
        return carry

    lax.fori_loop(0, nch, finish, 0)


def _rwkv(rkv, small, conv_w, w0, w_up, a0, a_up, g_up, k_k, k_a, r_k, norm_g, norm_b):
    bsz, s, _ = rkv.shape
    vec = lambda t: t.astype(F32).reshape(1, BRANCH_W)
    f = lambda t: t.astype(F32)
    return pl.pallas_call(
        _rwkv_body,
        out_shape=jax.ShapeDtypeStruct((bsz, s, BRANCH_W), BF16),
        grid=(bsz,),
        in_specs=[pl.BlockSpec((1, s, W_RKV), lambda b: (b, 0, 0)),
                  pl.BlockSpec((1, s, W_SMALL), lambda b: (b, 0, 0)),
                  _resident((3, 3 * BRANCH_W)), _resident((2, BRANCH_W)), _resident((2, RW_RANK, BRANCH_W)),
                  _resident((2, BRANCH_W)), _resident((2, RW_RANK, BRANCH_W)), _resident((2 * RW_RANK, BRANCH_W)),
                  _resident((1, BRANCH_W)), _resident((1, BRANCH_W)), _resident((1, BRANCH_W)),
                  _resident((1, BRANCH_W)), _resident((1, BRANCH_W))],
        out_specs=pl.BlockSpec((1, s, BRANCH_W), lambda b: (b, 0, 0)),
        scratch_shapes=[pltpu.VMEM((2, s, BRANCH_W), F32), pltpu.VMEM((s, BRANCH_W), F32), pltpu.VMEM((s, BRANCH_W), F32),
                        pltpu.VMEM((N_STREAMS, HEAD_DIM, HEAD_DIM), F32)],
        compiler_params=_params(("parallel",)),
        name="rwkv7",
    )(rkv, small, f(conv_w), f(w0), w_up.astype(BF16), f(a0), a_up.astype(BF16), g_up.astype(BF16), vec(k_k), vec(k_a),
      vec(r_k), vec(norm_g), vec(norm_b))


def _trunk(x, p, w_in_perm):
    bsz, s, d = x.shape
    n = bsz * s
    seq = lambda t: t.reshape(bsz, s, t.shape[-1])
    flat = lambda t: t.reshape(n, t.shape[-1])
    xf = _ln_rows(x.reshape(n, d), p["ln0_g"], p["ln0_b"])
    for l in range(DEPTH):
        na, mlqkv, mlo, rkv, small, merge = _in_proj(xf, w_in_perm[l])
        o_a = _na_attention(seq(na), _na_bias_table(p["na_rpb"][l], s // GRID_W))
        o_b = _mlstm(seq(mlqkv), seq(mlo), seq(small), p["ml_conv"][l], p["ml_gate_b"][l], p["ml_norm_g"][l])
        o_c = _rwkv(seq(rkv), seq(small), p["rw_conv"][l], p["rw_w0"][l], p["rw_w_up"][l], p["rw_a0"][l], p["rw_a_up"][l],
                    p["rw_g_up"][l], p["rw_k_k"][l], p["rw_k_a"][l], p["rw_r_k"][l], p["rw_norm_g"][l], p["rw_norm_b"][l])
        xf = _mix_ln(xf, flat(o_a), flat(o_b), flat(o_c), merge, p["w_br_a"][l], p["w_br_b"][l], p["w_br_c"][l],
                     p["w_out"][l], p["ln1_g"][l], p["ln1_b"][l])
        xf = _moe_ln(xf, p["w_router"], p["router_bias"], p["moe_w1"][l], p["moe_w3"][l], p["moe_w2"][l],
                     p["ln2_g"][l], p["ln2_b"][l])
    return xf.reshape(bsz, s, d)


def kernel(x_prompt, x_sample, ln0_g, ln0_b, w_in, na_rpb, ml_conv, ml_gate_b, ml_norm_g, rw_conv, rw_w0, rw_w_up,
           rw_a0, rw_a_up, rw_g_up, rw_k_k, rw_k_a, rw_r_k, rw_norm_g, rw_norm_b, w_br_a, w_br_b, w_br_c, w_out,
           ln1_g, ln1_b, w_router, router_bias, moe_w1, moe_w3, moe_w2, ln2_g, ln2_b):
    p = {
        "ln0_g": ln0_g, "ln0_b": ln0_b, "na_rpb": na_rpb, "ml_conv": ml_conv,
        "ml_gate_b": ml_gate_b, "ml_norm_g": ml_norm_g, "rw_conv": rw_conv, "rw_w0": rw_w0,
        "rw_w_up": rw_w_up, "rw_a0": rw_a0, "rw_a_up": rw_a_up, "rw_g_up": rw_g_up, "rw_k_k": rw_k_k,
        "rw_k_a": rw_k_a, "rw_r_k": rw_r_k, "rw_norm_g": rw_norm_g, "rw_norm_b": rw_norm_b,
        "w_br_a": w_br_a, "w_br_b": w_br_b, "w_br_c": w_br_c, "w_out": w_out, "ln1_g": ln1_g,
        "ln1_b": ln1_b, "w_router": w_router, "router_bias": router_bias, "moe_w1": moe_w1,
        "moe_w3": moe_w3, "moe_w2": moe_w2, "ln2_g": ln2_g, "ln2_b": ln2_b,
    }
    w_in_perm = [_permute_w_in(w_in[l]) for l in range(DEPTH)]
    return _trunk(x_prompt, p, w_in_perm), _trunk(x_sample, p, w_in_perm)
```

```python
import functools

import jax
import jax.numpy as jnp
import numpy as np
from jax import lax
from jax.experimental import pallas as pl
from jax.experimental.pallas import tpu as pltpu

F32 = jnp.float32
BF16 = jnp.bfloat16

D_MODEL = 1024
DEPTH = 4
GRID_W = 64
HEAD_DIM = 64
N_HEADS = 4
BRANCH_W = N_HEADS * HEAD_DIM
CHUNK = 64
NA_WIN_ROWS = 8
NA_WIN_COLS = 16
NA_ROW_CLASSES = 8
ML_NORM_EPS = 1e-6
RW_RANK = 32
RW_DECAY_SCALE = 0.606531
RW_NORM_EPS = 64e-5
N_EXPERTS = 16
EXPERTS_PER_GROUP = 4
D_EXPERT = 512
ALPHA = (2 * DEPTH) ** 0.25
LN_EPS = 1e-5
NEG = -1e30

W_NA = 3 * BRANCH_W
W_MLQKV = 3 * BRANCH_W
W_MLO = BRANCH_W
W_RKV = 3 * BRANCH_W
W_SMALL = 256
W_MERGE = 3 * D_MODEL
PIECES = (W_NA, W_MLQKV, W_MLO, W_RKV, W_SMALL, W_MERGE)
D_INP = sum(PIECES)
SMALL_GATE_OFF = 192

VMEM_LIMIT = 56 * 1024 * 1024
TOKEN_TILE = 512


def _params(sem):
    return pltpu.CompilerParams(dimension_semantics=sem, vmem_limit_bytes=VMEM_LIMIT)


def _resident(shape):
    nd = len(shape)
    return pl.BlockSpec(shape, lambda *_: (0,) * nd, pipeline_mode=pl.Buffered(1))


def _ln(z, g, b):
    mu = jnp.mean(z, axis=-1, keepdims=True)
    zc = z - mu
    var = jnp.mean(zc * zc, axis=-1, keepdims=True)
    return zc * lax.rsqrt(var + LN_EPS) * g + b


def _ln_rows_body(x_ref, g_ref, b_ref, o_ref):
    o_ref[...] = _ln(x_ref[...], g_ref[...], b_ref[...])


def _ln_rows(x, g, b):
    n, d = x.shape
    tm = TOKEN_TILE
    return pl.pallas_call(
        _ln_rows_body,
        out_shape=jax.ShapeDtypeStruct((n, d), F32),
        grid=(n // tm,),
        in_specs=[pl.BlockSpec((tm, d), lambda i: (i, 0)), _resident((1, d)), _resident((1, d))],
        out_specs=pl.BlockSpec((tm, d), lambda i: (i, 0)),
        compiler_params=_params(("parallel",)),
        name="ln_rows",
    )(x, g.reshape(1, d), b.reshape(1, d))


def _in_proj_body(x_ref, w_ref, *o_refs):
    xb = x_ref[...].astype(BF16)
    off = 0
    for o_ref, width in zip(o_refs, PIECES):
        for c0 in range(0, width, 768):
            c1 = min(c0 + 768, width)
            o_ref[:, c0:c1] = jnp.dot(xb, w_ref[:, off + c0:off + c1], preferred_element_type=F32).astype(BF16)
        off += width


def _in_proj(x, w):
    n, d = x.shape
    tm = TOKEN_TILE
    return pl.pallas_call(
        _in_proj_body,
        out_shape=[jax.ShapeDtypeStruct((n, wd), BF16) for wd in PIECES],
        grid=(n // tm,),
        in_specs=[pl.BlockSpec((tm, d), lambda i: (i, 0)), _resident((d, D_INP))],
        out_specs=[pl.BlockSpec((tm, wd), lambda i: (i, 0)) for wd in PIECES],
        compiler_params=_params(("parallel",)),
        name="in_proj",
    )(x, w)


def _permute_w_in(w_in_l):
    na, mlqkv, mlo, mlg, rkv, wl, al, gl, merge = jnp.split(
        w_in_l, np.cumsum([768, 768, 256, 16, 768, 64, 64, 64])[:8].tolist(), axis=1)
    pad = jnp.zeros((w_in_l.shape[0], W_SMALL - 208), w_in_l.dtype)
    return jnp.concatenate([na, mlqkv, mlo, rkv, wl, al, gl, mlg, pad, merge], axis=1).astype(BF16)


def _mix_ln_body(x_ref, oa_ref, ob_ref, oc_ref, mg_ref, wa_ref, wb_ref, wc_ref, wo_ref, g_ref, b_ref, o_ref):
    mixed = None
    for j, (o_br, w_br) in enumerate(((oa_ref, wa_ref), (ob_ref, wb_ref), (oc_ref, wc_ref))):
        gate = jax.nn.sigmoid(mg_ref[:, j * D_MODEL:(j + 1) * D_MODEL].astype(F32))
        term = gate * jnp.dot(o_br[...], w_br[...], preferred_element_type=F32)
        mixed = term if mixed is None else mixed + term
    z = ALPHA * x_ref[...] + jnp.dot(mixed.astype(BF16), wo_ref[...], preferred_element_type=F32)
    o_ref[...] = _ln(z, g_ref[...], b_ref[...])


def _mix_ln(x, oa, ob, oc, merge, wa, wb, wc, wo, g, b):
    n, d = x.shape
    tm = TOKEN_TILE
    row = lambda wd: pl.BlockSpec((tm, wd), lambda i: (i, 0))
    return pl.pallas_call(
        _mix_ln_body,
        out_shape=jax.ShapeDtypeStruct((n, d), F32),
        grid=(n // tm,),
        in_specs=[row(d), row(BRANCH_W), row(BRANCH_W), row(BRANCH_W), row(W_MERGE),
                  _resident((BRANCH_W, d)), _resident((BRANCH_W, d)), _resident((BRANCH_W, d)), _resident((d, d)),
                  _resident((1, d)), _resident((1, d))],
        out_specs=row(d),
        compiler_params=_params(("parallel",)),
        name="mix_ln",
    )(x, oa, ob, oc, merge, wa.astype(BF16), wb.astype(BF16), wc.astype(BF16), wo.astype(BF16),
      g.reshape(1, d), b.reshape(1, d))


MOE_TILE = 1024


def _top2_sum(a, b, c, d):
    hi1, lo1 = jnp.maximum(a, b), jnp.minimum(a, b)
    hi2, lo2 = jnp.maximum(c, d), jnp.minimum(c, d)
    return jnp.maximum(hi1, hi2) + jnp.maximum(jnp.minimum(hi1, hi2), jnp.maximum(lo1, lo2))


def _router_gates(x, wrt, rbias):
    logits = lax.dot_general(wrt, x, (((1,), (1,)), ((), ())), precision=lax.Precision.HIGHEST,
                             preferred_element_type=F32)
    ex = jnp.exp(logits - jnp.max(logits, axis=0, keepdims=True))
    probs = ex / jnp.sum(ex, axis=0, keepdims=True)
    sel = probs + rbias
    rows = [sel[e:e + 1, :] for e in range(N_EXPERTS)]
    n_groups = N_EXPERTS // EXPERTS_PER_GROUP
    scores = [_top2_sum(*rows[EXPERTS_PER_GROUP * g:EXPERTS_PER_GROUP * (g + 1)]) for g in range(n_groups)]
    best, gidx = scores[0], jnp.zeros_like(scores[0], dtype=jnp.int32)
    for g in range(1, n_groups):
        better = scores[g] > best
        gidx = jnp.where(better, g, gidx)
        best = jnp.where(better, scores[g], best)
    picked = []
    for e in range(N_EXPERTS):
        g = e // EXPERTS_PER_GROUP
        rank = jnp.zeros_like(gidx)
        for e2 in range(EXPERTS_PER_GROUP * g, EXPERTS_PER_GROUP * (g + 1)):
            if e2 == e:
                continue
            ahead = (rows[e2] >= rows[e]) if e2 < e else (rows[e2] > rows[e])
            rank = rank + ahead.astype(jnp.int32)
        chosen = (gidx == g) & (rank < 2)
        picked.append(jnp.where(chosen, probs[e:e + 1, :], 0.0))
    total = picked[0]
    for p in picked[1:]:
        total = total + p
    slots = []
    for k in range(EXPERTS_PER_GROUP):
        acc = picked[k]
        for g in range(1, n_groups):
            acc = acc + picked[EXPERTS_PER_GROUP * g + k]
        slots.append(acc / total)
    return jnp.concatenate(slots, axis=0), gidx


def _moe_ln_body(x_ref, wrt_ref, rb_ref, tri_ref, w1_ref, w3_ref, w2_ref, g_ref, b_ref, o_ref,
                 xb_s, rank_s, gidx_s, slot_s, xs_s, ys_s, rs_s, gs_s, acc_s, cnt_s):
    e = pl.program_id(1)
    grp = e // EXPERTS_PER_GROUP
    k = e % EXPERTS_PER_GROUP
    tm = x_ref.shape[0]
    n_groups = N_EXPERTS // EXPERTS_PER_GROUP

    @pl.when(e == 0)
    def _():
        x = x_ref[...]
        xb_s[...] = x.astype(BF16)
        slots, gidx = _router_gates(x, wrt_ref[...], rb_ref[...])
        hi = slots.astype(BF16)
        slot_s[...] = jnp.concatenate([hi, (slots - hi.astype(F32)).astype(BF16)], axis=0)
        gidx_s[...] = gidx
        member = [jnp.where(gidx == g, 1.0, 0.0) for g in range(n_groups)]
        onehot = jnp.concatenate(member + [jnp.zeros((8 - n_groups, tm), F32)], axis=0)
        upto = jnp.dot(onehot.astype(BF16), tri_ref[...], preferred_element_type=F32)
        rank = -1.0
        for g in range(n_groups):
            rank = rank + member[g] * upto[g:g + 1, :]
            cnt_s[g] = jnp.sum(member[g]).astype(jnp.int32)
        rank_s[...] = rank
        acc_s[...] = jnp.zeros_like(acc_s)

    n_blocks = (cnt_s[grp] + MOE_BLOCK - 1) // MOE_BLOCK

    @pl.when(k == 0)
    def _():
        def compact(blk, carry):
            want = (lax.broadcasted_iota(jnp.int32, (MOE_BLOCK, tm), 0) + blk * MOE_BLOCK).astype(F32)
            pick = jnp.where((rank_s[...] == want) & (gidx_s[...] == grp), 1.0, 0.0).astype(BF16)
            rs_s[blk] = pick
            xs_s[blk] = jnp.dot(pick, xb_s[...], preferred_element_type=F32).astype(BF16)
            gs_s[blk] = _dot_nt(pick, slot_s[...])
            return carry

        lax.fori_loop(0, n_blocks, compact, 0)

    lane = lax.broadcasted_iota(jnp.int32, (MOE_BLOCK, 2 * EXPERTS_PER_GROUP), 1)

    def expert_block(blk, first):
        xs = xs_s[blk]
        h1 = jnp.dot(xs, w1_ref[0], preferred_element_type=F32)
        h3 = jnp.dot(xs, w3_ref[0], preferred_element_type=F32)
        y = jnp.dot((h1 * jax.nn.sigmoid(h1) * h3).astype(BF16), w2_ref[0], preferred_element_type=F32)
        gate = jnp.sum(jnp.where(lane % EXPERTS_PER_GROUP == k, gs_s[blk], 0.0), axis=1, keepdims=True)
        if first:
            ys_s[blk] = gate * y
        else:
            ys_s[blk] += gate * y

    @pl.when(k == 0)
    def _():
        lax.fori_loop(0, n_blocks, lambda blk, c: (expert_block(blk, True), c)[1], 0)

    @pl.when(k > 0)
    def _():
        lax.fori_loop(0, n_blocks, lambda blk, c: (expert_block(blk, False), c)[1], 0)

    @pl.when(k == EXPERTS_PER_GROUP - 1)
    def _():
        def scatter(blk, carry):
            acc_s[...] += _dot_tn(rs_s[blk], ys_s[blk].astype(BF16))
            return carry

        lax.fori_loop(0, n_blocks, scatter, 0)

    @pl.when(e == N_EXPERTS - 1)
    def _():
        o_ref[...] = _ln(ALPHA * x_ref[...] + acc_s[...], g_ref[...], b_ref[...])


MOE_BLOCK = 320


def _moe_ln(x, w_router, router_bias, w1, w3, w2, g, b):
    n, d = x.shape
    tm = MOE_TILE
    nb = pl.cdiv(tm, MOE_BLOCK)
    tri = (np.arange(tm)[:, None] <= np.arange(tm)[None, :])
    return pl.pallas_call(
        _moe_ln_body,
        out_shape=jax.ShapeDtypeStruct((n, d), F32),
        grid=(n // tm, N_EXPERTS),
        in_specs=[pl.BlockSpec((tm, d), lambda i, e: (i, 0)),
                  _resident((N_EXPERTS, d)), _resident((N_EXPERTS, 1)), _resident((tm, tm)),
                  pl.BlockSpec((1, d, D_EXPERT), lambda i, e: (e, 0, 0)),
                  pl.BlockSpec((1, d, D_EXPERT), lambda i, e: (e, 0, 0)),
                  pl.BlockSpec((1, D_EXPERT, d), lambda i, e: (e, 0, 0)),
                  _resident((1, d)), _resident((1, d))],
        out_specs=pl.BlockSpec((tm, d), lambda i, e: (i, 0)),
        scratch_shapes=[pltpu.VMEM((tm, d), BF16), pltpu.VMEM((1, tm), F32), pltpu.VMEM((1, tm), jnp.int32),
                        pltpu.VMEM((2 * EXPERTS_PER_GROUP, tm), BF16),
                        pltpu.VMEM((nb, MOE_BLOCK, d), BF16), pltpu.VMEM((nb, MOE_BLOCK, d), F32),
                        pltpu.VMEM((nb, MOE_BLOCK, tm), BF16), pltpu.VMEM((nb, MOE_BLOCK, 2 * EXPERTS_PER_GROUP), F32),
                        pltpu.VMEM((tm, d), F32), pltpu.SMEM((N_EXPERTS // EXPERTS_PER_GROUP,), jnp.int32)],
        compiler_params=_params(("parallel", "arbitrary")),
        name="moe_ln",
    )(x, w_router.T, router_bias.reshape(N_EXPERTS, 1), jnp.asarray(tri, BF16), w1.astype(BF16), w3.astype(BF16),
      w2.astype(BF16), g.reshape(1, d), b.reshape(1, d))


def _na_bias_table(rpb, rows):
    assert rows >= NA_WIN_ROWS
    cols = np.arange(GRID_W)
    win = np.clip(cols - NA_WIN_COLS // 2, 0, GRID_W - NA_WIN_COLS)
    rel = cols[None, :] - cols[:, None]
    ok = (cols[None, :] >= win[:, None]) & (cols[None, :] < win[:, None] + NA_WIN_COLS)
    pick = (np.arange(2 * NA_WIN_COLS - 1)[None, None, :] == (rel + NA_WIN_COLS - 1)[:, :, None]) & ok[:, :, None]
    t = jnp.einsum("hdr,ckr->hdck", rpb.astype(F32), jnp.asarray(pick, F32), precision=lax.Precision.HIGHEST)
    t = jnp.where(ok[None, None], t, NEG)
    per_class = [t[:, NA_WIN_ROWS - 1 - cls:2 * NA_WIN_ROWS - 1 - cls] for cls in range(NA_ROW_CLASSES)]
    t = jnp.stack(per_class, axis=1)
    return jnp.transpose(t, (0, 1, 3, 2, 4)).reshape(rpb.shape[0], NA_ROW_CLASSES, GRID_W, NA_WIN_ROWS * GRID_W)


NA_ROWS_PER_STEP = 2


def _na_body(qkv_ref, bias_ref, o_ref):
    rows = qkv_ref.shape[1] // GRID_W
    band = NA_WIN_ROWS * GRID_W

    head_cols = lambda base, h: slice(base + h * HEAD_DIM, base + (h + 1) * HEAD_DIM)

    def row_step(i, carry):
        units = []
        for u in range(NA_ROWS_PER_STEP):
            r = NA_ROWS_PER_STEP * i + u
            rs = jnp.clip(r - NA_WIN_ROWS // 2, 0, rows - NA_WIN_ROWS)
            cls = jnp.where(r < NA_WIN_ROWS // 2, r, NA_WIN_ROWS // 2 + jnp.maximum(r - (rows - NA_WIN_ROWS // 2), 0))
            q_rows = pl.ds(pl.multiple_of(r * GRID_W, GRID_W), GRID_W)
            k_rows = pl.ds(pl.multiple_of(rs * GRID_W, GRID_W), band)
            units += [(h, cls, q_rows, k_rows) for h in range(N_HEADS)]
        scores = [_dot_nt(qkv_ref[0, q_rows, head_cols(0, h)], qkv_ref[0, k_rows, head_cols(BRANCH_W, h)])
                  for h, _, q_rows, k_rows in units]
        maxes = [jnp.max(scores[n] * (HEAD_DIM ** -0.5) + bias_ref[h, cls], axis=1, keepdims=True)
                 for n, (h, cls, _, _) in enumerate(units)]
        probs = [jnp.exp(scores[n] * (HEAD_DIM ** -0.5) + bias_ref[h, cls] - maxes[n]) for n, (h, cls, _, _) in enumerate(units)]
        sums = [jnp.sum(p, axis=1, keepdims=True) for p in probs]
        outs = [jnp.dot(probs[n].astype(BF16), qkv_ref[0, k_rows, head_cols(2 * BRANCH_W, h)], preferred_element_type=F32)
                for n, (h, _, _, k_rows) in enumerate(units)]
        for n, (h, _, q_rows, _) in enumerate(units):
            o_ref[0, q_rows, head_cols(0, h)] = (outs[n] / sums[n]).astype(BF16)
        return carry

    lax.fori_loop(0, rows // NA_ROWS_PER_STEP, row_step, 0)


def _na_attention(qkv, bias):
    bsz, s, _ = qkv.shape
    return pl.pallas_call(
        _na_body,
        out_shape=jax.ShapeDtypeStruct((bsz, s, BRANCH_W), BF16),
        grid=(bsz,),
        in_specs=[pl.BlockSpec((1, s, W_NA), lambda b: (b, 0, 0)), _resident(bias.shape)],
        out_specs=pl.BlockSpec((1, s, BRANCH_W), lambda b: (b, 0, 0)),
        compiler_params=_params(("parallel",)),
        name="na_attention",
    )(qkv, bias)


def _eye(n):
    return lax.broadcasted_iota(jnp.int32, (n, n), 0) == lax.broadcasted_iota(jnp.int32, (n, n), 1)


def _row_to_col(row, eye):
    return jnp.sum(jnp.where(eye, row, 0.0), axis=1, keepdims=True)


def _conv3_rows(ref, c, nch, cols, w_ref):
    seq = nch * CHUNK
    r0 = pl.multiple_of(c * CHUNK, CHUNK)
    x = ref[0, pl.ds(r0, CHUNK), cols].astype(F32)
    lo = pl.multiple_of(jnp.maximum(r0 - 16, 0), 16)
    hi = pl.multiple_of(jnp.minimum(r0 + CHUNK, seq - 16), 16)
    prev = ref[0, pl.ds(lo, 16), cols][15:16, :].astype(F32) * jnp.where(c > 0, 1.0, 0.0)
    nxt = ref[0, pl.ds(hi, 16), cols][0:1, :].astype(F32) * jnp.where(c < nch - 1, 1.0, 0.0)
    rid = lax.broadcasted_iota(jnp.int32, x.shape, 0)
    xm = jnp.where(rid == 0, prev, pltpu.roll(x, 1, axis=0))
    xp = jnp.where(rid == CHUNK - 1, nxt, pltpu.roll(x, CHUNK - 1, axis=0))
    return w_ref[0:1, :] * xm + w_ref[1:2, :] * x + w_ref[2:3, :] * xp


def _head_norm(h, eps):
    mu = jnp.mean(h, axis=-1, keepdims=True)
    hc = h - mu
    return hc * lax.rsqrt(jnp.mean(hc * hc, axis=-1, keepdims=True) + eps)


def _head_mean_matrix():
    hi = lax.broadcasted_iota(jnp.int32, (BRANCH_W, BRANCH_W), 0) // HEAD_DIM
    hj = lax.broadcasted_iota(jnp.int32, (BRANCH_W, BRANCH_W), 1) // HEAD_DIM
    return jnp.where(hi == hj, 1.0 / HEAD_DIM, 0.0).astype(BF16)


def _split_dot(x, m):
    hi = x.astype(BF16)
    lo = (x - hi.astype(F32)).astype(BF16)
    return jnp.dot(hi, m, preferred_element_type=F32) + jnp.dot(lo, m, preferred_element_type=F32)


def _head_norm_wide(y, head_mean, eps):
    yc = y - _split_dot(y, head_mean)
    return yc * lax.rsqrt(_split_dot(yc * yc, head_mean) + eps)


def _dot_nt(a, b, **kw):
    return lax.dot_general(a, b, (((1,), (1,)), ((), ())), preferred_element_type=F32, **kw)


def _dot_tn(a, b, **kw):
    return lax.dot_general(a, b, (((0,), (0,)), ((), ())), preferred_element_type=F32, **kw)


N_STREAMS = 2 * N_HEADS


ML_CHUNKS_PER_STEP = 2
STATE_W = 2 * HEAD_DIM


def _mlstm_body(qkv_ref, o_ref_in, g_ref, gb_ref, cw_ref, ng_ref, out_ref,
                qk_s, vx_s, ig_s, b_s, ct_s, mc_s, bl_s, mprev_s, cst_s, mst_s):
    seq = qkv_ref.shape[1]
    nch = seq // CHUNK
    eye = _eye(CHUNK)
    ri = lax.broadcasted_iota(jnp.int32, (CHUNK, CHUNK), 0)
    ci = lax.broadcasted_iota(jnp.int32, (CHUNK, CHUNK), 1)
    head_cols = lambda base, h: slice(base + h * HEAD_DIM, base + (h + 1) * HEAD_DIM)
    head_mean = _head_mean_matrix()
    ones_col =jnp.where(lax.broadcasted_iota(jnp.int32, (CHUNK, HEAD_DIM), 1) == 0, 1.0, 0.0).astype(BF16)

    def conv_step(c, carry):
        rows = pl.ds(pl.multiple_of(c * CHUNK, CHUNK), CHUNK)
        y = _conv3_rows(qkv_ref, c, nch, slice(0, 2 * BRANCH_W), cw_ref)
        y = y * jax.nn.sigmoid(y)
        scale = jnp.where(lax.broadcasted_iota(jnp.int32, (1, 2 * BRANCH_W), 1) < BRANCH_W, 1.0, HEAD_DIM ** -0.5)
        qk_s[rows, :] = (y * scale).astype(BF16)
        for h in range(N_HEADS):
            vx_s[rows, h * STATE_W:h * STATE_W + HEAD_DIM] = qkv_ref[0, rows, head_cols(2 * BRANCH_W, h)]
            vx_s[rows, h * STATE_W + HEAD_DIM:(h + 1) * STATE_W] = ones_col
        return carry

    lax.fori_loop(0, nch, conv_step, 0)

    for j in range(N_STREAMS):
        rev, h = j >= N_HEADS, j % N_HEADS
        gi = (2 * N_HEADS if rev else 0) + h
        ig_s[j] = g_ref[0, gi] + gb_ref[gi]
        fpre = g_ref[0, gi + N_HEADS] + gb_ref[gi + N_HEADS]
        lf = jnp.minimum(fpre, 0.0) - jnp.log1p(jnp.exp(-jnp.abs(fpre)))
        cum = (ri >= ci) if rev else (ri <= ci)
        b_s[j] = jnp.dot(lf, cum.astype(F32), precision=lax.Precision.HIGHEST, preferred_element_type=F32)
        cst_s[j] = jnp.zeros((HEAD_DIM, STATE_W), F32)
        mst_s[j] = jnp.full((1, 128), NEG, F32)

    def chunk_stats(c, carry):
        rows = pl.ds(pl.multiple_of(c * CHUNK, CHUNK), CHUNK)
        a_rs, mcs = [], []
        for j in range(N_STREAMS):
            b_r = b_s[j, pl.ds(c, 1), :]
            bl = b_r[:, 0:1] if j >= N_HEADS else b_r[:, CHUNK - 1:CHUNK]
            a_rs.append(bl - b_r + ig_s[j, pl.ds(c, 1), :])
            bl_s[j, c] = jnp.broadcast_to(bl, (1, 128))
        for j in range(N_STREAMS):
            mcs.append(jnp.max(a_rs[j], axis=1, keepdims=True))
            mc_s[j, c] = jnp.broadcast_to(mcs[j], (1, 128))
        diags = [jnp.where(eye, jnp.exp(a_rs[j] - mcs[j]), 0.0).astype(BF16) for j in range(N_STREAMS)]
        wvs = [jnp.dot(diags[j], vx_s[rows, (j % N_HEADS) * STATE_W:(j % N_HEADS + 1) * STATE_W],
                       preferred_element_type=F32).astype(BF16) for j in range(N_STREAMS)]
        for j in range(N_STREAMS):
            ct_s[j, c] = _dot_tn(qk_s[rows, head_cols(BRANCH_W, j % N_HEADS)], wvs[j])
        return carry

    lax.fori_loop(0, nch, chunk_stats, 0)

    def scan_step(i, carry):
        for j in range(N_STREAMS):
            c = (nch - 1 - i) if j >= N_HEADS else i
            ct, m = cst_s[j], mst_s[j]
            bl, mc = bl_s[j, c], mc_s[j, c]
            m_new = jnp.maximum(bl + m, mc)
            cst_s[j] = jnp.exp(bl + m - m_new) * ct + jnp.exp(mc - m_new) * ct_s[j, c]
            mst_s[j] = m_new
            ct_s[j, c] = ct
            mprev_s[j, c] = m
        return carry

    lax.fori_loop(0, nch, scan_step, 0)

    sub = lax.broadcasted_iota(jnp.int32, (16, CHUNK), 0)
    sub_w = lax.broadcasted_iota(jnp.int32, (16, 2 * CHUNK), 0)
    lane_w = lax.broadcasted_iota(jnp.int32, (16, 2 * CHUNK), 1)
    first_lane = jnp.where(lax.broadcasted_iota(jnp.int32, (1, CHUNK), 1) == 0, 1.0, 0.0)
    rhs_fixed = jnp.where((sub_w < 2) & (lane_w < CHUNK + 2), 1.0,
                          jnp.where((sub_w >= 4) & (sub_w < 6) & (lane_w == CHUNK + 1), -1.0, 0.0))
    pad = jnp.full((1, CHUNK), NEG, F32)

    def hi_lo(x):
        hi = x.astype(BF16).astype(F32)
        return hi, x - hi

    def chunk_out(i, carry):
        chunks = [ML_CHUNKS_PER_STEP * i + u for u in range(ML_CHUNKS_PER_STEP)]
        rows = [pl.ds(pl.multiple_of(c * CHUNK, CHUNK), CHUNK) for c in chunks]
        units = [(u, j) for u in range(ML_CHUNKS_PER_STEP) for j in range(N_STREAMS)]
        qs = [[qk_s[r, head_cols(0, h)] for h in range(N_HEADS)] for r in rows]
        vs = [[vx_s[r, h * STATE_W:(h + 1) * STATE_W] for h in range(N_HEADS)] for r in rows]
        qks = [[_dot_nt(qs[u][h], qk_s[r, head_cols(BRANCH_W, h)]) for h in range(N_HEADS)] for u, r in enumerate(rows)]
        inter_parts = [jnp.dot(qs[u][j % N_HEADS], ct_s[j, chunks[u]].astype(BF16), preferred_element_type=F32)
                       for u, j in units]
        b_rs = [b_s[j, pl.ds(chunks[u], 1), :] for u, j in units]
        d_rs = [ig_s[j, pl.ds(chunks[u], 1), :] - b_rs[n] for n, (u, j) in enumerate(units)]
        m_prevs = [mprev_s[j, chunks[u]][:, 0:1] for u, j in units]
        run = jnp.concatenate([jnp.concatenate([d, pad], axis=1) for d in d_rs], axis=0)
        fwd_rows = lax.broadcasted_iota(jnp.int32, run.shape, 0) % N_STREAMS < N_HEADS
        shift = 1
        while shift < CHUNK:
            moved = jnp.where(fwd_rows, pltpu.roll(run, shift, axis=1), pltpu.roll(run, 2 * CHUNK - shift, axis=1))
            run = jnp.maximum(run, moved)
            shift *= 2
        exps = []
        for n in range(len(units)):
            u_hi, u_lo = hi_lo(-jnp.maximum(m_prevs[n], run[n:n + 1, :CHUNK]))
            b_hi, b_lo = hi_lo(b_rs[n])
            mid_hi, mid_lo = hi_lo(jnp.concatenate([d_rs[n], m_prevs[n] * first_lane], axis=1))
            lhs = jnp.where(sub == 0, u_hi, jnp.where(sub == 1, u_lo, jnp.where((sub == 2) | (sub == 3), 1.0,
                            jnp.where(sub == 4, b_hi, jnp.where(sub == 5, b_lo, 0.0)))))
            rhs = jnp.where(sub_w == 2, mid_hi, jnp.where(sub_w == 3, mid_lo, rhs_fixed))
            exps.append(jnp.exp(_dot_tn(lhs.astype(BF16), rhs.astype(BF16))))
        w_intras = []
        for n, (u, j) in enumerate(units):
            tri = (ci >= ri) if j >= N_HEADS else (ci <= ri)
            w_intras.append((jnp.where(tri, exps[n][:, :CHUNK], 0.0) * qks[u][j % N_HEADS]).astype(BF16))
        intra_parts = [jnp.dot(w_intras[n], vs[u][j % N_HEADS], preferred_element_type=F32) for n, (u, j) in enumerate(units)]
        hs = []
        for n in range(len(units)):
            inter, clamp = exps[n][:, CHUNK:CHUNK + 1], exps[n][:, CHUNK + 1:CHUNK + 2]
            both = inter * inter_parts[n] + intra_parts[n]
            den = both[:, HEAD_DIM:HEAD_DIM + 1]
            hs.append(both[:, :HEAD_DIM] / jnp.maximum(jnp.abs(den), clamp))
        tots = [jnp.concatenate([hs[u * N_STREAMS + h] + hs[u * N_STREAMS + N_HEADS + h] for h in range(N_HEADS)], axis=1)
                for u in range(ML_CHUNKS_PER_STEP)]
        centred = [t - _split_dot(t, head_mean) for t in tots]
        variances = [_split_dot(t * t, head_mean) for t in centred]
        for u in range(ML_CHUNKS_PER_STEP):
            tot = centred[u] * lax.rsqrt(variances[u] + ML_NORM_EPS) * ng_ref[...]
            out_ref[0, rows[u], :] = (tot * jax.nn.sigmoid(o_ref_in[0, rows[u], :].astype(F32))).astype(BF16)
        return carry

    lax.fori_loop(0, nch // ML_CHUNKS_PER_STEP, chunk_out, 0)


def _mlstm(qkv, o_pre, small, conv_w, gate_b, norm_g):
    bsz, s, _ = qkv.shape
    nch = s // CHUNK
    n_gate = 4 * N_HEADS
    gates = small[..., SMALL_GATE_OFF:SMALL_GATE_OFF + n_gate].astype(F32)
    gates = jnp.transpose(gates, (0, 2, 1)).reshape(bsz, n_gate, nch, CHUNK)
    gate_b = jnp.broadcast_to(gate_b.astype(F32).reshape(n_gate, 1, 1), (n_gate, 1, CHUNK))
    row = lambda shape: pltpu.VMEM(shape, F32)
    return pl.pallas_call(
        _mlstm_body,
        out_shape=jax.ShapeDtypeStruct((bsz, s, BRANCH_W), BF16),
        grid=(bsz,),
        in_specs=[pl.BlockSpec((1, s, W_MLQKV), lambda b: (b, 0, 0)),
                  pl.BlockSpec((1, s, BRANCH_W), lambda b: (b, 0, 0)),
                  pl.BlockSpec((1, n_gate, nch, CHUNK), lambda b: (b, 0, 0, 0)),
                  _resident((n_gate, 1, CHUNK)), _resident((3, 2 * BRANCH_W)), _resident((1, BRANCH_W))],
        out_specs=pl.BlockSpec((1, s, BRANCH_W), lambda b: (b, 0, 0)),
        scratch_shapes=[pltpu.VMEM((s, 2 * BRANCH_W), BF16), pltpu.VMEM((s, N_HEADS * STATE_W), BF16),
                        row((N_STREAMS, nch, CHUNK)), row((N_STREAMS, nch, CHUNK)),
                        row((N_STREAMS, nch, HEAD_DIM, STATE_W)),
                        row((N_STREAMS, nch, 1, 128)), row((N_STREAMS, nch, 1, 128)), row((N_STREAMS, nch, 1, 128)),
                        row((N_STREAMS, HEAD_DIM, STATE_W)), row((N_STREAMS, 1, 128))],
        compiler_params=_params(("parallel",)),
        name="mlstm",
    )(qkv, o_pre, gates, gate_b, conv_w.astype(F32), norm_g.astype(F32).reshape(1, BRANCH_W))


RW_CHUNKS_PER_STEP = 4


def _rwkv_body(rkv_ref, sm_ref, cw_ref, w0_ref, wup_ref, a0_ref, aup_ref, gup_ref, kk_ref, ka_ref, rk_ref, ng_ref, nb_ref,
               out_ref, y_s, bonus_s, gate_s, st_s):
    seq = rkv_ref.shape[1]
    nch = seq // CHUNK
    ri = lax.broadcasted_iota(jnp.int32, (CHUNK, CHUNK), 0)
    ci = lax.broadcasted_iota(jnp.int32, (CHUNK, CHUNK), 1)
    ident = (ri == ci).astype(F32)

    bdot = lambda x, y: jnp.dot(x.astype(BF16), y.astype(BF16), preferred_element_type=F32)

    def low_rank(x, d, up_ref, bias_ref):
        return bias_ref[d:d + 1, :] + bdot(x[:, d * RW_RANK:(d + 1) * RW_RANK], up_ref[d])

    st_s[...] = jnp.zeros_like(st_s)

    def chunk_step(i, carry):
        streams = []
        for rev, u in [(rev, u) for rev in (False, True) for u in range(RW_CHUNKS_PER_STEP)]:
            d = 1 if rev else 0
            strict = (ci > ri) if rev else (ci < ri)
            incl = (ci >= ri) if rev else (ci <= ri)
            last = slice(0, 1) if rev else slice(CHUNK - 1, CHUNK)
            c = (nch - 1 - (RW_CHUNKS_PER_STEP * i + u)) if rev else (RW_CHUNKS_PER_STEP * i + u)
            rows = pl.ds(pl.multiple_of(c * CHUNK, CHUNK), CHUNK)
            rkv = _conv3_rows(rkv_ref, c, nch, slice(0, 3 * BRANCH_W), cw_ref)
            r, k, v = rkv[:, :BRANCH_W], rkv[:, BRANCH_W:2 * BRANCH_W], rkv[:, 2 * BRANCH_W:]
            sm = sm_ref[0, rows, :].astype(F32)
            w_lo, a_lo, g_lo = jnp.tanh(sm[:, :2 * RW_RANK]), sm[:, 2 * RW_RANK:4 * RW_RANK], sm[:, 4 * RW_RANK:6 * RW_RANK]
            lw = -RW_DECAY_SCALE * jax.nn.sigmoid(low_rank(w_lo, d, wup_ref, w0_ref))
            a = jax.nn.sigmoid(low_rank(a_lo, d, aup_ref, a0_ref))
            kd = k * (1.0 + (a - 1.0) * ka_ref[...])
            kk_raw = k * kk_ref[...]
            lw_hi = lw.astype(BF16)
            lw_lo = (lw - lw_hi.astype(F32)).astype(BF16)
            cum = jnp.dot(incl.astype(BF16), jnp.concatenate([lw_hi, lw_lo], axis=1), preferred_element_type=F32)
            lp_inc = cum[:, :BRANCH_W] + cum[:, BRANCH_W:]
            p_end = jnp.exp(lp_inc[last, :])
            e_inc = jnp.exp(lp_inc)
            e_inv = jnp.exp(-lp_inc)
            e_exc = jnp.exp(lp_inc - lw)
            if not rev:
                a_b = jax.nn.sigmoid(low_rank(a_lo, 1, aup_ref, a0_ref))
                kd_sum = kd + k * (1.0 + (a_b - 1.0) * ka_ref[...])
                gate_s[rows, :] = bdot(jax.nn.sigmoid(g_lo), gup_ref[...])
            for h in range(N_HEADS):
                hc = slice(h * HEAD_DIM, (h + 1) * HEAD_DIM)
                kk_h = kk_raw[:, hc]
                kk_h = kk_h / jnp.maximum(jnp.sqrt(jnp.sum(kk_h * kk_h, axis=1, keepdims=True)), 1e-12)
                streams.append(dict(
                    j=d * N_HEADS + h, d=d, rows=rows, hc=hc, strict=strict, incl=incl, p_end=p_end[:, hc],
                    b_hat=(kk_h * a[:, hc] * e_inv[:, hc]).astype(BF16), k_hat=(kd[:, hc] * e_inv[:, hc]).astype(BF16),
                    kk_t=kk_h * e_exc[:, hc], r_t=r[:, hc] * e_inc[:, hc], v_h=v[:, hc].astype(BF16)))
                if not rev:
                    bonus_s[rows, hc] = jnp.sum(r[:, hc] * kd_sum[:, hc] * rk_ref[:, hc], axis=1, keepdims=True) * v[:, hc]

        for s in streams:
            s["p4"] = _dot_nt(jnp.concatenate([s["kk_t"], s["r_t"]], axis=0).astype(BF16),
                              jnp.concatenate([s["b_hat"], s["k_hat"]], axis=0))
            s["pw"] = [jnp.where(s["strict"], -s["p4"][:CHUNK, :CHUNK], 0.0).astype(BF16)]
            s["pair"] = []
        for step in range(1, 6):
            for s in streams:
                s["pw"].append(jnp.dot(s["pw"][-1], s["pw"][-1], preferred_element_type=F32).astype(BF16))
            if step % 2 == 1:
                for s in streams:
                    lo, hi = s["pw"][step - 1], s["pw"][step]
                    s["pair"].append(ident + lo.astype(F32) + hi.astype(F32) + jnp.dot(lo, hi, preferred_element_type=F32))
        for s in streams:
            s["inv"] = bdot(s["pair"][0], s["pair"][1])
        for s in streams:
            s["inv"] = bdot(s["inv"], s["pair"][2])
        for s in streams:
            p4 = s["p4"]
            masked = jnp.concatenate([jnp.where(s["strict"], p4[:CHUNK, CHUNK:], 0.0),
                                      jnp.where(s["incl"], p4[CHUNK:, CHUNK:], 0.0)], axis=0)
            s["akv_rkv"] = bdot(masked, s["v_h"])
        for s in streams:
            s["m12"] = bdot(s["inv"], jnp.concatenate([s["kk_t"], s["akv_rkv"][:CHUNK]], axis=1)).astype(BF16)
        for s in streams:
            s["gh"] = _dot_tn(s["m12"], s["b_hat"])
            s["vk"] = _dot_tn(s["v_h"], s["k_hat"])
        for s in streams:
            s["qy"] = bdot(jnp.where(s["incl"], s["p4"][CHUNK:, :CHUNK], 0.0), s["m12"])
        for s in streams:
            g_mat = (ident - s["gh"][:HEAD_DIM]) * s["p_end"]
            h_mat = (s["vk"] - s["gh"][HEAD_DIM:]) * s["p_end"]
            q_mat = s["r_t"] - s["qy"][:, :HEAD_DIM]
            y0 = s["akv_rkv"][CHUNK:] - s["qy"][:, HEAD_DIM:]
            s0 = st_s[s["j"]].astype(BF16)
            y_s[s["d"], s["rows"], s["hc"]] = _dot_nt(q_mat.astype(BF16), s0) + y0
            st_s[s["j"]] = bdot(s0, g_mat) + h_mat
        return carry

    lax.fori_loop(0, nch // RW_CHUNKS_PER_STEP, chunk_step, 0)

    head_mean = _head_mean_matrix()

    def finish(i, carry):
        rows = [pl.ds(pl.multiple_of((RW_CHUNKS_PER_STEP * i + u) * CHUNK, CHUNK), CHUNK) for u in range(RW_CHUNKS_PER_STEP)]
        ys = [y_s[0, r, :] + y_s[1, r, :] for r in rows]
        ycs = [y - _split_dot(y, head_mean) for y in ys]
        vrs = [_split_dot(yc * yc, head_mean) for yc in ycs]
        for r, yc, vr in zip(rows, ycs, vrs):
            tot = yc * lax.rsqrt(vr + RW_NORM_EPS) * ng_ref[...] + nb_ref[...]
            out_ref[0, r, :] = ((tot + bonus_s[r, :]) * gate_s[r, :]).astype(BF16)
        return carry

    lax.fori_loop(0, nch // RW_CHUNKS_PER_STEP, finish, 0)


def _rwkv(rkv, small, conv_w, w0, w_up, a0, a_up, g_up, k_k, k_a, r_k, norm_g, norm_b):
    bsz, s, _ = rkv.shape
    vec = lambda t: t.astype(F32).reshape(1, BRANCH_W)
    f = lambda t: t.astype(F32)
    return pl.pallas_call(
        _rwkv_body,
        out_shape=jax.ShapeDtypeStruct((bsz, s, BRANCH_W), BF16),
        grid=(bsz,),
        in_specs=[pl.BlockSpec((1, s, W_RKV), lambda b: (b, 0, 0)),
                  pl.BlockSpec((1, s, W_SMALL), lambda b: (b, 0, 0)),
                  _resident((3, 3 * BRANCH_W)), _resident((2, BRANCH_W)), _resident((2, RW_RANK, BRANCH_W)),
                  _resident((2, BRANCH_W)), _resident((2, RW_RANK, BRANCH_W)), _resident((2 * RW_RANK, BRANCH_W)),
                  _resident((1, BRANCH_W)), _resident((1, BRANCH_W)), _resident((1, BRANCH_W)),
                  _resident((1, BRANCH_W)), _resident((1, BRANCH_W))],
        out_specs=pl.BlockSpec((1, s, BRANCH_W), lambda b: (b, 0, 0)),
        scratch_shapes=[pltpu.VMEM((2, s, BRANCH_W), F32), pltpu.VMEM((s, BRANCH_W), F32), pltpu.VMEM((s, BRANCH_W), F32),
                        pltpu.VMEM((N_STREAMS, HEAD_DIM, HEAD_DIM), F32)],
        compiler_params=_params(("parallel",)),
        name="rwkv7",
    )(rkv, small, f(conv_w), f(w0), w_up.astype(BF16), f(a0), a_up.astype(BF16), g_up.astype(BF16), vec(k_k), vec(k_a),
      vec(r_k), vec(norm_g), vec(norm_b))


def _trunk(x, p, w_in_perm):
    bsz, s, d = x.shape
    n = bsz * s
    seq = lambda t: t.reshape(bsz, s, t.shape[-1])
    flat = lambda t: t.reshape(n, t.shape[-1])
    xf = _ln_rows(x.reshape(n, d), p["ln0_g"], p["ln0_b"])
    for l in range(DEPTH):
        na, mlqkv, mlo, rkv, small, merge = _in_proj(xf, w_in_perm[l])
        o_a = _na_attention(seq(na), _na_bias_table(p["na_rpb"][l], s // GRID_W))
        o_b = _mlstm(seq(mlqkv), seq(mlo), seq(small), p["ml_conv"][l], p["ml_gate_b"][l], p["ml_norm_g"][l])
        o_c = _rwkv(seq(rkv), seq(small), p["rw_conv"][l], p["rw_w0"][l], p["rw_w_up"][l], p["rw_a0"][l], p["rw_a_up"][l],
                    p["rw_g_up"][l], p["rw_k_k"][l], p["rw_k_a"][l], p["rw_r_k"][l], p["rw_norm_g"][l], p["rw_norm_b"][l])
        xf = _mix_ln(xf, flat(o_a), flat(o_b), flat(o_c), merge, p["w_br_a"][l], p["w_br_b"][l], p["w_br_c"][l],
                     p["w_out"][l], p["ln1_g"][l], p["ln1_b"][l])
        xf = _moe_ln(xf, p["w_router"], p["router_bias"], p["moe_w1"][l], p["moe_w3"][l], p["moe_w2"][l],
                     p["ln2_g"][l], p["ln2_b"][l])
    return xf.reshape(bsz, s, d)


def kernel(x_prompt, x_sample, ln0_g, ln0_b, w_in, na_rpb, ml_conv, ml_gate_b, ml_norm_g, rw_conv, rw_w0, rw_w_up,
           rw_a0, rw_a_up, rw_g_up, rw_k_k, rw_k_a, rw_r_k, rw_norm_g, rw_norm_b, w_br_a, w_br_b, w_br_c, w_out,
           ln1_g, ln1_b, w_router, router_bias, moe_w1, moe_w3, moe_w2, ln2_g, ln2_b):
    p = {
        "ln0_g": ln0_g, "ln0_b": ln0_b, "na_rpb": na_rpb, "ml_conv": ml_conv,
        "ml_gate_b": ml_gate_b, "ml_norm_g": ml_norm_g, "rw_conv": rw_conv, "rw_w0": rw_w0,
        "rw_w_up": rw_w_up, "rw_a0": rw_a0, "rw_a_up": rw_a_up, "rw_g_up": rw_g_up, "rw_k_k": rw_k_k,
        "rw_k_a": rw_k_a, "rw_r_k": rw_r_k, "rw_norm_g": rw_norm_g, "rw_norm_b": rw_norm_b,
        "w_br_a": w_br_a, "w_br_b": w_br_b, "w_br_c": w_br_c, "w_out": w_out, "ln1_g": ln1_g,
        "ln1_b": ln1_b, "w_router": w_router, "router_bias": router_bias, "moe_w1": moe_w1,
        "moe_w3": moe_w3, "moe_w2": moe_w2, "ln2_g": ln2_g, "ln2_b": ln2_b,
    }
    w_in_perm = [_permute_w_in(w_in[l]) for l in range(DEPTH)]
    return _trunk(x_prompt, p, w_in_perm), _trunk(x_sample, p, w_in_perm)
```

```python
import functools

import jax
import jax.numpy as jnp
import numpy as np
from jax import lax
from jax.experimental import pallas as pl
from jax.experimental.pallas import tpu as pltpu

F32 = jnp.float32
BF16 = jnp.bfloat16

D_MODEL = 1024
DEPTH = 4
GRID_W = 64
HEAD_DIM = 64
N_HEADS = 4
BRANCH_W = N_HEADS * HEAD_DIM
CHUNK = 64
NA_WIN_ROWS = 8
NA_WIN_COLS = 16
NA_ROW_CLASSES = 8
ML_NORM_EPS = 1e-6
RW_RANK = 32
RW_DECAY_SCALE = 0.606531
RW_NORM_EPS = 64e-5
N_EXPERTS = 16
EXPERTS_PER_GROUP = 4
D_EXPERT = 512
ALPHA = (2 * DEPTH) ** 0.25
LN_EPS = 1e-5
NEG = -1e30

W_NA = 3 * BRANCH_W
W_MLQKV = 3 * BRANCH_W
W_MLO = BRANCH_W
W_RKV = 3 * BRANCH_W
W_SMALL = 256
W_MERGE = 3 * D_MODEL
PIECES = (W_NA, W_MLQKV, W_MLO, W_RKV, W_SMALL, W_MERGE)
D_INP = sum(PIECES)
SMALL_GATE_OFF = 192

VMEM_LIMIT = 56 * 1024 * 1024
TOKEN_TILE = 512


def _params(sem):
    return pltpu.CompilerParams(dimension_semantics=sem, vmem_limit_bytes=VMEM_LIMIT)


def _resident(shape):
    nd = len(shape)
    return pl.BlockSpec(shape, lambda *_: (0,) * nd, pipeline_mode=pl.Buffered(1))


def _ln(z, g, b):
    mu = jnp.mean(z, axis=-1, keepdims=True)
    zc = z - mu
    var = jnp.mean(zc * zc, axis=-1, keepdims=True)
    return zc * lax.rsqrt(var + LN_EPS) * g + b


def _ln_rows_body(x_ref, g_ref, b_ref, o_ref):
    o_ref[...] = _ln(x_ref[...], g_ref[...], b_ref[...])


def _ln_rows(x, g, b):
    n, d = x.shape
    tm = TOKEN_TILE
    return pl.pallas_call(
        _ln_rows_body,
        out_shape=jax.ShapeDtypeStruct((n, d), F32),
        grid=(n // tm,),
        in_specs=[pl.BlockSpec((tm, d), lambda i: (i, 0)), _resident((1, d)), _resident((1, d))],
        out_specs=pl.BlockSpec((tm, d), lambda i: (i, 0)),
        compiler_params=_params(("parallel",)),
        name="ln_rows",
    )(x, g.reshape(1, d), b.reshape(1, d))


def _in_proj_body(x_ref, w_ref, *o_refs):
    xb = x_ref[...].astype(BF16)
    off = 0
    for o_ref, width in zip(o_refs, PIECES):
        for c0 in range(0, width, 768):
            c1 = min(c0 + 768, width)
            o_ref[:, c0:c1] = jnp.dot(xb, w_ref[:, off + c0:off + c1], preferred_element_type=F32).astype(BF16)
        off += width


def _in_proj(x, w):
    n, d = x.shape
    tm = TOKEN_TILE
    return pl.pallas_call(
        _in_proj_body,
        out_shape=[jax.ShapeDtypeStruct((n, wd), BF16) for wd in PIECES],
        grid=(n // tm,),
        in_specs=[pl.BlockSpec((tm, d), lambda i: (i, 0)), _resident((d, D_INP))],
        out_specs=[pl.BlockSpec((tm, wd), lambda i: (i, 0)) for wd in PIECES],
        compiler_params=_params(("parallel",)),
        name="in_proj",
    )(x, w)


def _permute_w_in(w_in_l):
    na, mlqkv, mlo, mlg, rkv, wl, al, gl, merge = jnp.split(
        w_in_l, np.cumsum([768, 768, 256, 16, 768, 64, 64, 64])[:8].tolist(), axis=1)
    pad = jnp.zeros((w_in_l.shape[0], W_SMALL - 208), w_in_l.dtype)
    return jnp.concatenate([na, mlqkv, mlo, rkv, wl, al, gl, mlg, pad, merge], axis=1).astype(BF16)


def _mix_ln_body(x_ref, oa_ref, ob_ref, oc_ref, mg_ref, wa_ref, wb_ref, wc_ref, wo_ref, g_ref, b_ref, o_ref):
    mixed = None
    for j, (o_br, w_br) in enumerate(((oa_ref, wa_ref), (ob_ref, wb_ref), (oc_ref, wc_ref))):
        gate = jax.nn.sigmoid(mg_ref[:, j * D_MODEL:(j + 1) * D_MODEL].astype(F32))
        term = gate * jnp.dot(o_br[...], w_br[...], preferred_element_type=F32)
        mixed = term if mixed is None else mixed + term
    z = ALPHA * x_ref[...] + jnp.dot(mixed.astype(BF16), wo_ref[...], preferred_element_type=F32)
    o_ref[...] = _ln(z, g_ref[...], b_ref[...])


def _mix_ln(x, oa, ob, oc, merge, wa, wb, wc, wo, g, b):
    n, d = x.shape
    tm = TOKEN_TILE
    row = lambda wd: pl.BlockSpec((tm, wd), lambda i: (i, 0))
    return pl.pallas_call(
        _mix_ln_body,
        out_shape=jax.ShapeDtypeStruct((n, d), F32),
        grid=(n // tm,),
        in_specs=[row(d), row(BRANCH_W), row(BRANCH_W), row(BRANCH_W), row(W_MERGE),
                  _resident((BRANCH_W, d)), _resident((BRANCH_W, d)), _resident((BRANCH_W, d)), _resident((d, d)),
                  _resident((1, d)), _resident((1, d))],
        out_specs=row(d),
        compiler_params=_params(("parallel",)),
        name="mix_ln",
    )(x, oa, ob, oc, merge, wa.astype(BF16), wb.astype(BF16), wc.astype(BF16), wo.astype(BF16),
      g.reshape(1, d), b.reshape(1, d))


MOE_TILE = 1024


def _top2_sum(a, b, c, d):
    hi1, lo1 = jnp.maximum(a, b), jnp.minimum(a, b)
    hi2, lo2 = jnp.maximum(c, d), jnp.minimum(c, d)
    return jnp.maximum(hi1, hi2) + jnp.maximum(jnp.minimum(hi1, hi2), jnp.maximum(lo1, lo2))


def _router_gates(x, wrt, rbias):
    w_hi, x_hi = wrt.astype(BF16), x.astype(BF16)
    w_lo, x_lo = (wrt - w_hi.astype(F32)).astype(BF16), (x - x_hi.astype(F32)).astype(BF16)
    logits = _dot_nt(w_hi, x_hi) + (_dot_nt(w_hi, x_lo) + _dot_nt(w_lo, x_hi))
    ex = jnp.exp(logits - jnp.max(logits, axis=0, keepdims=True))
    probs = ex / jnp.sum(ex, axis=0, keepdims=True)
    sel = probs + rbias
    rows = [sel[e:e + 1, :] for e in range(N_EXPERTS)]
    n_groups = N_EXPERTS // EXPERTS_PER_GROUP
    scores = [_top2_sum(*rows[EXPERTS_PER_GROUP * g:EXPERTS_PER_GROUP * (g + 1)]) for g in range(n_groups)]
    best, gidx = scores[0], jnp.zeros_like(scores[0], dtype=jnp.int32)
    for g in range(1, n_groups):
        better = scores[g] > best
        gidx = jnp.where(better, g, gidx)
        best = jnp.where(better, scores[g], best)
    picked = []
    for e in range(N_EXPERTS):
        g = e // EXPERTS_PER_GROUP
        rank = jnp.zeros_like(gidx)
        for e2 in range(EXPERTS_PER_GROUP * g, EXPERTS_PER_GROUP * (g + 1)):
            if e2 == e:
                continue
            ahead = (rows[e2] >= rows[e]) if e2 < e else (rows[e2] > rows[e])
            rank = rank + ahead.astype(jnp.int32)
        chosen = (gidx == g) & (rank < 2)
        picked.append(jnp.where(chosen, probs[e:e + 1, :], 0.0))
    total = picked[0]
    for p in picked[1:]:
        total = total + p
    slots = []
    for k in range(EXPERTS_PER_GROUP):
        acc = picked[k]
        for g in range(1, n_groups):
            acc = acc + picked[EXPERTS_PER_GROUP * g + k]
        slots.append(acc / total)
    return jnp.concatenate(slots, axis=0), gidx


def _moe_ln_body(x_ref, wrt_ref, rb_ref, tri_ref, w1_ref, w3_ref, w2_ref, g_ref, b_ref, o_ref,
                 xb_s, rank_s, gidx_s, slot_s, acc_s, cnt_s):
    grp = pl.program_id(1)
    tm = x_ref.shape[0]
    n_groups = N_EXPERTS // EXPERTS_PER_GROUP

    @pl.when(grp == 0)
    def _():
        x = x_ref[...]
        xb_s[...] = x.astype(BF16)
        slots, gidx = _router_gates(x, wrt_ref[...], rb_ref[...])
        hi = slots.astype(BF16)
        slot_s[...] = jnp.concatenate([hi, (slots - hi.astype(F32)).astype(BF16)], axis=0)
        gidx_s[...] = gidx
        member = [jnp.where(gidx == g, 1.0, 0.0) for g in range(n_groups)]
        onehot = jnp.concatenate(member + [jnp.zeros((8 - n_groups, tm), F32)], axis=0)
        upto = jnp.dot(onehot.astype(BF16), tri_ref[...], preferred_element_type=F32)
        rank = -1.0
        for g in range(n_groups):
            rank = rank + member[g] * upto[g:g + 1, :]
            cnt_s[g] = jnp.sum(member[g]).astype(jnp.int32)
        rank_s[...] = rank
        acc_s[...] = jnp.zeros_like(acc_s)

    n_blocks = (cnt_s[grp] + MOE_BLOCK - 1) // MOE_BLOCK

    def block(blk, carry):
        want = (lax.broadcasted_iota(jnp.int32, (MOE_BLOCK, tm), 0) + blk * MOE_BLOCK).astype(F32)
        pick = jnp.where((rank_s[...] == want) & (gidx_s[...] == grp), 1.0, 0.0).astype(BF16)
        xs = jnp.dot(pick, xb_s[...], preferred_element_type=F32).astype(BF16)
        gates = _dot_nt(pick, slot_s[...])
        gates = gates[:, :EXPERTS_PER_GROUP] + gates[:, EXPERTS_PER_GROUP:]
        h1s = [jnp.dot(xs, w1_ref[k], preferred_element_type=F32) for k in range(EXPERTS_PER_GROUP)]
        h3s = [jnp.dot(xs, w3_ref[k], preferred_element_type=F32) for k in range(EXPERTS_PER_GROUP)]
        hs = [(h1s[k] * jax.nn.sigmoid(h1s[k]) * h3s[k] * gates[:, k:k + 1]).astype(BF16) for k in range(EXPERTS_PER_GROUP)]
        ys = jnp.dot(hs[0], w2_ref[0], preferred_element_type=F32)
        for k in range(1, EXPERTS_PER_GROUP):
            ys = ys + jnp.dot(hs[k], w2_ref[k], preferred_element_type=F32)
        acc_s[...] += _dot_tn(pick, ys.astype(BF16))
        return carry

    lax.fori_loop(0, n_blocks, block, 0)

    @pl.when(grp == n_groups - 1)
    def _():
        o_ref[...] = _ln(ALPHA * x_ref[...] + acc_s[...], g_ref[...], b_ref[...])


MOE_BLOCK = 320


def _moe_ln(x, w_router, router_bias, w1, w3, w2, g, b):
    n, d = x.shape
    tm = MOE_TILE
    tri = (np.arange(tm)[:, None] <= np.arange(tm)[None, :])
    grp_w = lambda shape: pl.BlockSpec((EXPERTS_PER_GROUP,) + shape, lambda i, g: (g, 0, 0))
    return pl.pallas_call(
        _moe_ln_body,
        out_shape=jax.ShapeDtypeStruct((n, d), F32),
        grid=(n // tm, N_EXPERTS // EXPERTS_PER_GROUP),
        in_specs=[pl.BlockSpec((tm, d), lambda i, g: (i, 0)),
                  _resident((N_EXPERTS, d)), _resident((N_EXPERTS, 1)), _resident((tm, tm)),
                  grp_w((d, D_EXPERT)), grp_w((d, D_EXPERT)), grp_w((D_EXPERT, d)),
                  _resident((1, d)), _resident((1, d))],
        out_specs=pl.BlockSpec((tm, d), lambda i, g: (i, 0)),
        scratch_shapes=[pltpu.VMEM((tm, d), BF16), pltpu.VMEM((1, tm), F32), pltpu.VMEM((1, tm), jnp.int32),
                        pltpu.VMEM((2 * EXPERTS_PER_GROUP, tm), BF16),
                        pltpu.VMEM((tm, d), F32), pltpu.SMEM((N_EXPERTS // EXPERTS_PER_GROUP,), jnp.int32)],
        compiler_params=_params(("parallel", "arbitrary")),
        name="moe_ln",
    )(x, w_router.T, router_bias.reshape(N_EXPERTS, 1), jnp.asarray(tri, BF16), w1.astype(BF16), w3.astype(BF16),
      w2.astype(BF16), g.reshape(1, d), b.reshape(1, d))


def _na_bias_table(rpb, rows):
    assert rows >= NA_WIN_ROWS
    cols = np.arange(GRID_W)
    win = np.clip(cols - NA_WIN_COLS // 2, 0, GRID_W - NA_WIN_COLS)
    rel = cols[None, :] - cols[:, None]
    ok = (cols[None, :] >= win[:, None]) & (cols[None, :] < win[:, None] + NA_WIN_COLS)
    pick = (np.arange(2 * NA_WIN_COLS - 1)[None, None, :] == (rel + NA_WIN_COLS - 1)[:, :, None]) & ok[:, :, None]
    t = jnp.einsum("hdr,ckr->hdck", rpb.astype(F32), jnp.asarray(pick, F32), precision=lax.Precision.HIGHEST)
    t = jnp.where(ok[None, None], t, NEG)
    per_class = [t[:, NA_WIN_ROWS - 1 - cls:2 * NA_WIN_ROWS - 1 - cls] for cls in range(NA_ROW_CLASSES)]
    t = jnp.stack(per_class, axis=1)
    return jnp.transpose(t, (0, 1, 3, 2, 4)).reshape(rpb.shape[0], NA_ROW_CLASSES, GRID_W, NA_WIN_ROWS * GRID_W)


NA_ROWS_PER_STEP = 2


def _na_body(qkv_ref, bias_ref, o_ref, vx_s):
    rows = qkv_ref.shape[1] // GRID_W
    band = NA_WIN_ROWS * GRID_W

    head_cols = lambda base, h: slice(base + h * HEAD_DIM, base + (h + 1) * HEAD_DIM)
    ones_col = jnp.where(lax.broadcasted_iota(jnp.int32, (GRID_W, HEAD_DIM), 1) == 0, 1.0, 0.0).astype(BF16)

    def extend(r, carry):
        q_rows = pl.ds(pl.multiple_of(r * GRID_W, GRID_W), GRID_W)
        for h in range(N_HEADS):
            vx_s[q_rows, 2 * h * HEAD_DIM:(2 * h + 1) * HEAD_DIM] = qkv_ref[0, q_rows, head_cols(2 * BRANCH_W, h)]
            vx_s[q_rows, (2 * h + 1) * HEAD_DIM:(2 * h + 2) * HEAD_DIM] = ones_col
        return carry

    lax.fori_loop(0, rows, extend, 0)

    def row_step(i, carry):
        units = []
        for u in range(NA_ROWS_PER_STEP):
            r = NA_ROWS_PER_STEP * i + u
            rs = jnp.clip(r - NA_WIN_ROWS // 2, 0, rows - NA_WIN_ROWS)
            cls = jnp.where(r < NA_WIN_ROWS // 2, r, NA_WIN_ROWS // 2 + jnp.maximum(r - (rows - NA_WIN_ROWS // 2), 0))
            q_rows = pl.ds(pl.multiple_of(r * GRID_W, GRID_W), GRID_W)
            k_rows = pl.ds(pl.multiple_of(rs * GRID_W, GRID_W), band)
            units += [(h, cls, q_rows, k_rows) for h in range(N_HEADS)]
        scores = [_dot_nt(qkv_ref[0, q_rows, head_cols(0, h)], qkv_ref[0, k_rows, head_cols(BRANCH_W, h)])
                  for h, _, q_rows, k_rows in units]
        maxes = [jnp.max(scores[n] * (HEAD_DIM ** -0.5) + bias_ref[h, cls], axis=1, keepdims=True)
                 for n, (h, cls, _, _) in enumerate(units)]
        probs = [jnp.exp(scores[n] * (HEAD_DIM ** -0.5) + bias_ref[h, cls] - maxes[n]) for n, (h, cls, _, _) in enumerate(units)]
        outs = [jnp.dot(probs[n].astype(BF16), vx_s[k_rows, 2 * h * HEAD_DIM:(2 * h + 2) * HEAD_DIM], preferred_element_type=F32)
                for n, (h, _, _, k_rows) in enumerate(units)]
        for n, (h, _, q_rows, _) in enumerate(units):
            o_ref[0, q_rows, head_cols(0, h)] = (outs[n][:, :HEAD_DIM] / outs[n][:, HEAD_DIM:HEAD_DIM + 1]).astype(BF16)
        return carry

    lax.fori_loop(0, rows // NA_ROWS_PER_STEP, row_step, 0)


def _na_attention(qkv, bias):
    bsz, s, _ = qkv.shape
    return pl.pallas_call(
        _na_body,
        out_shape=jax.ShapeDtypeStruct((bsz, s, BRANCH_W), BF16),
        grid=(bsz,),
        in_specs=[pl.BlockSpec((1, s, W_NA), lambda b: (b, 0, 0)), _resident(bias.shape)],
        out_specs=pl.BlockSpec((1, s, BRANCH_W), lambda b: (b, 0, 0)),
        scratch_shapes=[pltpu.VMEM((s, 2 * BRANCH_W), BF16)],
        compiler_params=_params(("parallel",)),
        name="na_attention",
    )(qkv, bias)


def _eye(n):
    return lax.broadcasted_iota(jnp.int32, (n, n), 0) == lax.broadcasted_iota(jnp.int32, (n, n), 1)


def _row_to_col(row, eye):
    return jnp.sum(jnp.where(eye, row, 0.0), axis=1, keepdims=True)


def _conv3_rows(ref, c, nch, cols, w_ref):
    seq = nch * CHUNK
    r0 = pl.multiple_of(c * CHUNK, CHUNK)
    x = ref[0, pl.ds(r0, CHUNK), cols].astype(F32)
    lo = pl.multiple_of(jnp.maximum(r0 - 16, 0), 16)
    hi = pl.multiple_of(jnp.minimum(r0 + CHUNK, seq - 16), 16)
    prev = ref[0, pl.ds(lo, 16), cols][15:16, :].astype(F32) * jnp.where(c > 0, 1.0, 0.0)
    nxt = ref[0, pl.ds(hi, 16), cols][0:1, :].astype(F32) * jnp.where(c < nch - 1, 1.0, 0.0)
    rid = lax.broadcasted_iota(jnp.int32, x.shape, 0)
    xm = jnp.where(rid == 0, prev, pltpu.roll(x, 1, axis=0))
    xp = jnp.where(rid == CHUNK - 1, nxt, pltpu.roll(x, CHUNK - 1, axis=0))
    return w_ref[0:1, :] * xm + w_ref[1:2, :] * x + w_ref[2:3, :] * xp


def _head_norm(h, eps):
    mu = jnp.mean(h, axis=-1, keepdims=True)
    hc = h - mu
    return hc * lax.rsqrt(jnp.mean(hc * hc, axis=-1, keepdims=True) + eps)


def _head_mean_matrix():
    hi = lax.broadcasted_iota(jnp.int32, (BRANCH_W, BRANCH_W), 0) // HEAD_DIM
    hj = lax.broadcasted_iota(jnp.int32, (BRANCH_W, BRANCH_W), 1) // HEAD_DIM
    return jnp.where(hi == hj, 1.0 / HEAD_DIM, 0.0).astype(BF16)


def _split_dot(x, m):
    hi = x.astype(BF16)
    lo = (x - hi.astype(F32)).astype(BF16)
    return jnp.dot(hi, m, preferred_element_type=F32) + jnp.dot(lo, m, preferred_element_type=F32)


def _head_norm_wide(y, head_mean, eps):
    yc = y - _split_dot(y, head_mean)
    return yc * lax.rsqrt(_split_dot(yc * yc, head_mean) + eps)


def _dot_nt(a, b, **kw):
    return lax.dot_general(a, b, (((1,), (1,)), ((), ())), preferred_element_type=F32, **kw)


def _dot_tn(a, b, **kw):
    return lax.dot_general(a, b, (((0,), (0,)), ((), ())), preferred_element_type=F32, **kw)


N_STREAMS = 2 * N_HEADS


ML_CHUNKS_PER_STEP = 4
STATE_W = 2 * HEAD_DIM


def _mlstm_body(qkv_ref, o_ref_in, g_ref, gb_ref, cw_ref, ng_ref, out_ref,
                qk_s, vx_s, ig_s, b_s, ct_s, mc_s, bl_s, mprev_s, cst_s, mst_s):
    seq = qkv_ref.shape[1]
    nch = seq // CHUNK
    eye = _eye(CHUNK)
    ri = lax.broadcasted_iota(jnp.int32, (CHUNK, CHUNK), 0)
    ci = lax.broadcasted_iota(jnp.int32, (CHUNK, CHUNK), 1)
    head_cols = lambda base, h: slice(base + h * HEAD_DIM, base + (h + 1) * HEAD_DIM)
    head_mean = _head_mean_matrix()
    ones_col =jnp.where(lax.broadcasted_iota(jnp.int32, (CHUNK, HEAD_DIM), 1) == 0, 1.0, 0.0).astype(BF16)

    def conv_step(c, carry):
        rows = pl.ds(pl.multiple_of(c * CHUNK, CHUNK), CHUNK)
        y = _conv3_rows(qkv_ref, c, nch, slice(0, 2 * BRANCH_W), cw_ref)
        y = y * jax.nn.sigmoid(y)
        scale = jnp.where(lax.broadcasted_iota(jnp.int32, (1, 2 * BRANCH_W), 1) < BRANCH_W, 1.0, HEAD_DIM ** -0.5)
        qk_s[rows, :] = (y * scale).astype(BF16)
        for h in range(N_HEADS):
            vx_s[rows, h * STATE_W:h * STATE_W + HEAD_DIM] = qkv_ref[0, rows, head_cols(2 * BRANCH_W, h)]
            vx_s[rows, h * STATE_W + HEAD_DIM:(h + 1) * STATE_W] = ones_col
        return carry

    lax.fori_loop(0, nch, conv_step, 0)

    for j in range(N_STREAMS):
        rev, h = j >= N_HEADS, j % N_HEADS
        gi = (2 * N_HEADS if rev else 0) + h
        ig_s[j] = g_ref[0, gi] + gb_ref[gi]
        fpre = g_ref[0, gi + N_HEADS] + gb_ref[gi + N_HEADS]
        lf = jnp.minimum(fpre, 0.0) - jnp.log1p(jnp.exp(-jnp.abs(fpre)))
        cum = (ri >= ci) if rev else (ri <= ci)
        b_s[j] = jnp.dot(lf, cum.astype(F32), precision=lax.Precision.HIGHEST, preferred_element_type=F32)
        cst_s[j] = jnp.zeros((HEAD_DIM, STATE_W), F32)
        mst_s[j] = jnp.full((1, 128), NEG, F32)

    def chunk_stats(i, carry):
        chunks = [ML_CHUNKS_PER_STEP * i + u for u in range(ML_CHUNKS_PER_STEP)]
        rows = [pl.ds(pl.multiple_of(c * CHUNK, CHUNK), CHUNK) for c in chunks]
        units = [(chunks[u], rows[u], j) for u in range(ML_CHUNKS_PER_STEP) for j in range(N_STREAMS)]
        a_rs, mcs = [], []
        for c, _, j in units:
            b_r = b_s[j, pl.ds(c, 1), :]
            bl = b_r[:, 0:1] if j >= N_HEADS else b_r[:, CHUNK - 1:CHUNK]
            a_rs.append(bl - b_r + ig_s[j, pl.ds(c, 1), :])
            bl_s[j, c] = jnp.broadcast_to(bl, (1, 128))
        for n, (c, _, j) in enumerate(units):
            mcs.append(jnp.max(a_rs[n], axis=1, keepdims=True))
            mc_s[j, c] = jnp.broadcast_to(mcs[n], (1, 128))
        diags = [jnp.where(eye, jnp.exp(a_rs[n] - mcs[n]), 0.0).astype(BF16) for n in range(len(units))]
        wvs = [jnp.dot(diags[n], vx_s[r, (j % N_HEADS) * STATE_W:(j % N_HEADS + 1) * STATE_W],
                       preferred_element_type=F32).astype(BF16) for n, (_, r, j) in enumerate(units)]
        for n, (c, r, j) in enumerate(units):
            ct_s[j, c] = _dot_tn(qk_s[r, head_cols(BRANCH_W, j % N_HEADS)], wvs[n])
        return carry

    lax.fori_loop(0, nch // ML_CHUNKS_PER_STEP, chunk_stats, 0)

    def scan_step(i, carry):
        for j in range(N_STREAMS):
            c = (nch - 1 - i) if j >= N_HEADS else i
            ct, m = cst_s[j], mst_s[j]
            bl, mc = bl_s[j, c], mc_s[j, c]
            m_new = jnp.maximum(bl + m, mc)
            cst_s[j] = jnp.exp(bl + m - m_new) * ct + jnp.exp(mc - m_new) * ct_s[j, c]
            mst_s[j] = m_new
            ct_s[j, c] = ct
            mprev_s[j, c] = m
        return carry

    lax.fori_loop(0, nch, scan_step, 0)

    sub = lax.broadcasted_iota(jnp.int32, (16, CHUNK), 0)
    sub_w = lax.broadcasted_iota(jnp.int32, (16, 2 * CHUNK), 0)
    lane_w = lax.broadcasted_iota(jnp.int32, (16, 2 * CHUNK), 1)
    first_lane = jnp.where(lax.broadcasted_iota(jnp.int32, (1, CHUNK), 1) == 0, 1.0, 0.0)
    rhs_fixed = jnp.where((sub_w < 2) & (lane_w < CHUNK + 2), 1.0,
                          jnp.where((sub_w >= 4) & (sub_w < 6) & (lane_w == CHUNK + 1), -1.0, 0.0))
    pad = jnp.full((1, CHUNK), NEG, F32)

    def hi_lo(x):
        hi = x.astype(BF16).astype(F32)
        return hi, x - hi

    def chunk_out(i, carry):
        chunks = [ML_CHUNKS_PER_STEP * i + u for u in range(ML_CHUNKS_PER_STEP)]
        rows = [pl.ds(pl.multiple_of(c * CHUNK, CHUNK), CHUNK) for c in chunks]
        units = [(u, j) for u in range(ML_CHUNKS_PER_STEP) for j in range(N_STREAMS)]
        qs = [[qk_s[r, head_cols(0, h)] for h in range(N_HEADS)] for r in rows]
        vs = [[vx_s[r, h * STATE_W:(h + 1) * STATE_W] for h in range(N_HEADS)] for r in rows]
        qks = [[_dot_nt(qs[u][h], qk_s[r, head_cols(BRANCH_W, h)]) for h in range(N_HEADS)] for u, r in enumerate(rows)]
        inter_parts = [jnp.dot(qs[u][j % N_HEADS], ct_s[j, chunks[u]].astype(BF16), preferred_element_type=F32)
                       for u, j in units]
        b_rs = [b_s[j, pl.ds(chunks[u], 1), :] for u, j in units]
        d_rs = [ig_s[j, pl.ds(chunks[u], 1), :] - b_rs[n] for n, (u, j) in enumerate(units)]
        m_prevs = [mprev_s[j, chunks[u]][:, 0:1] for u, j in units]
        run = jnp.concatenate([jnp.concatenate([d, pad], axis=1) for d in d_rs], axis=0)
        fwd_rows = lax.broadcasted_iota(jnp.int32, run.shape, 0) % N_STREAMS < N_HEADS
        shift = 1
        while shift < CHUNK:
            moved = jnp.where(fwd_rows, pltpu.roll(run, shift, axis=1), pltpu.roll(run, 2 * CHUNK - shift, axis=1))
            run = jnp.maximum(run, moved)
            shift *= 2
        exps = []
        for n in range(len(units)):
            u_hi, u_lo = hi_lo(-jnp.maximum(m_prevs[n], run[n:n + 1, :CHUNK]))
            b_hi, b_lo = hi_lo(b_rs[n])
            mid_hi, mid_lo = hi_lo(jnp.concatenate([d_rs[n], m_prevs[n] * first_lane], axis=1))
            lhs = jnp.where(sub == 0, u_hi, jnp.where(sub == 1, u_lo, jnp.where((sub == 2) | (sub == 3), 1.0,
                            jnp.where(sub == 4, b_hi, jnp.where(sub == 5, b_lo, 0.0)))))
            rhs = jnp.where(sub_w == 2, mid_hi, jnp.where(sub_w == 3, mid_lo, rhs_fixed))
            exps.append(jnp.exp(_dot_tn(lhs.astype(BF16), rhs.astype(BF16))))
        w_intras = []
        for n, (u, j) in enumerate(units):
            tri = (ci >= ri) if j >= N_HEADS else (ci <= ri)
            w_intras.append((jnp.where(tri, exps[n][:, :CHUNK], 0.0) * qks[u][j % N_HEADS]).astype(BF16))
        intra_parts = [jnp.dot(w_intras[n], vs[u][j % N_HEADS], preferred_element_type=F32) for n, (u, j) in enumerate(units)]
        hs = []
        for n in range(len(units)):
            inter, clamp = exps[n][:, CHUNK:CHUNK + 1], exps[n][:, CHUNK + 1:CHUNK + 2]
            both = inter * inter_parts[n] + intra_parts[n]
            den = both[:, HEAD_DIM:HEAD_DIM + 1]
            hs.append(both[:, :HEAD_DIM] / jnp.maximum(jnp.abs(den), clamp))
        tots = [jnp.concatenate([hs[u * N_STREAMS + h] + hs[u * N_STREAMS + N_HEADS + h] for h in range(N_HEADS)], axis=1)
                for u in range(ML_CHUNKS_PER_STEP)]
        centred = [t - _split_dot(t, head_mean) for t in tots]
        variances = [_split_dot(t * t, head_mean) for t in centred]
        for u in range(ML_CHUNKS_PER_STEP):
            tot = centred[u] * lax.rsqrt(variances[u] + ML_NORM_EPS) * ng_ref[...]
            out_ref[0, rows[u], :] = (tot * jax.nn.sigmoid(o_ref_in[0, rows[u], :].astype(F32))).astype(BF16)
        return carry

    lax.fori_loop(0, nch // ML_CHUNKS_PER_STEP, chunk_out, 0)


def _mlstm(qkv, o_pre, small, conv_w, gate_b, norm_g):
    bsz, s, _ = qkv.shape
    nch = s // CHUNK
    n_gate = 4 * N_HEADS
    gates = small[..., SMALL_GATE_OFF:SMALL_GATE_OFF + n_gate].astype(F32)
    gates = jnp.transpose(gates, (0, 2, 1)).reshape(bsz, n_gate, nch, CHUNK)
    gate_b = jnp.broadcast_to(gate_b.astype(F32).reshape(n_gate, 1, 1), (n_gate, 1, CHUNK))
    row = lambda shape: pltpu.VMEM(shape, F32)
    return pl.pallas_call(
        _mlstm_body,
        out_shape=jax.ShapeDtypeStruct((bsz, s, BRANCH_W), BF16),
        grid=(bsz,),
        in_specs=[pl.BlockSpec((1, s, W_MLQKV), lambda b: (b, 0, 0)),
                  pl.BlockSpec((1, s, BRANCH_W), lambda b: (b, 0, 0)),
                  pl.BlockSpec((1, n_gate, nch, CHUNK), lambda b: (b, 0, 0, 0)),
                  _resident((n_gate, 1, CHUNK)), _resident((3, 2 * BRANCH_W)), _resident((1, BRANCH_W))],
        out_specs=pl.BlockSpec((1, s, BRANCH_W), lambda b: (b, 0, 0)),
        scratch_shapes=[pltpu.VMEM((s, 2 * BRANCH_W), BF16), pltpu.VMEM((s, N_HEADS * STATE_W), BF16),
                        row((N_STREAMS, nch, CHUNK)), row((N_STREAMS, nch, CHUNK)),
                        row((N_STREAMS, nch, HEAD_DIM, STATE_W)),
                        row((N_STREAMS, nch, 1, 128)), row((N_STREAMS, nch, 1, 128)), row((N_STREAMS, nch, 1, 128)),
                        row((N_STREAMS, HEAD_DIM, STATE_W)), row((N_STREAMS, 1, 128))],
        compiler_params=_params(("parallel",)),
        name="mlstm",
    )(qkv, o_pre, gates, gate_b, conv_w.astype(F32), norm_g.astype(F32).reshape(1, BRANCH_W))


RW_CHUNKS_PER_STEP = 4


def _rwkv_body(rkv_ref, sm_ref, cw_ref, w0_ref, wup_ref, a0_ref, aup_ref, gup_ref, kk_ref, ka_ref, rk_ref, ng_ref, nb_ref,
               out_ref, y_s, bonus_s, gate_s, st_s):
    seq = rkv_ref.shape[1]
    nch = seq // CHUNK
    ri = lax.broadcasted_iota(jnp.int32, (CHUNK, CHUNK), 0)
    ci = lax.broadcasted_iota(jnp.int32, (CHUNK, CHUNK), 1)
    ident = (ri == ci).astype(F32)

    bdot = lambda x, y: jnp.dot(x.astype(BF16), y.astype(BF16), preferred_element_type=F32)

    def low_rank(x, d, up_ref, bias_ref):
        return bias_ref[d:d + 1, :] + bdot(x[:, d * RW_RANK:(d + 1) * RW_RANK], up_ref[d])

    st_s[...] = jnp.zeros_like(st_s)

    def chunk_step(i, carry):
        streams = []
        for rev, u in [(rev, u) for rev in (False, True) for u in range(RW_CHUNKS_PER_STEP)]:
            d = 1 if rev else 0
            strict = (ci > ri) if rev else (ci < ri)
            incl = (ci >= ri) if rev else (ci <= ri)
            last = slice(0, 1) if rev else slice(CHUNK - 1, CHUNK)
            c = (nch - 1 - (RW_CHUNKS_PER_STEP * i + u)) if rev else (RW_CHUNKS_PER_STEP * i + u)
            rows = pl.ds(pl.multiple_of(c * CHUNK, CHUNK), CHUNK)
            rkv = _conv3_rows(rkv_ref, c, nch, slice(0, 3 * BRANCH_W), cw_ref)
            r, k, v = rkv[:, :BRANCH_W], rkv[:, BRANCH_W:2 * BRANCH_W], rkv[:, 2 * BRANCH_W:]
            sm = sm_ref[0, rows, :].astype(F32)
            w_lo, a_lo, g_lo = jnp.tanh(sm[:, :2 * RW_RANK]), sm[:, 2 * RW_RANK:4 * RW_RANK], sm[:, 4 * RW_RANK:6 * RW_RANK]
            lw = -RW_DECAY_SCALE * jax.nn.sigmoid(low_rank(w_lo, d, wup_ref, w0_ref))
            a = jax.nn.sigmoid(low_rank(a_lo, d, aup_ref, a0_ref))
            kd = k * (1.0 + (a - 1.0) * ka_ref[...])
            kk_raw = k * kk_ref[...]
            lw_hi = lw.astype(BF16)
            lw_lo = (lw - lw_hi.astype(F32)).astype(BF16)
            cum = jnp.dot(incl.astype(BF16), jnp.concatenate([lw_hi, lw_lo], axis=1), preferred_element_type=F32)
            lp_inc = cum[:, :BRANCH_W] + cum[:, BRANCH_W:]
            p_end = jnp.exp(lp_inc[last, :])
            e_inc = jnp.exp(lp_inc)
            e_inv = jnp.exp(-lp_inc)
            e_exc = jnp.exp(lp_inc - lw)
            if not rev:
                a_b = jax.nn.sigmoid(low_rank(a_lo, 1, aup_ref, a0_ref))
                kd_sum = kd + k * (1.0 + (a_b - 1.0) * ka_ref[...])
                gate_s[rows, :] = bdot(jax.nn.sigmoid(g_lo), gup_ref[...])
            for h in range(N_HEADS):
                hc = slice(h * HEAD_DIM, (h + 1) * HEAD_DIM)
                kk_h = kk_raw[:, hc]
                kk_h = kk_h / jnp.maximum(jnp.sqrt(jnp.sum(kk_h * kk_h, axis=1, keepdims=True)), 1e-12)
                streams.append(dict(
                    j=d * N_HEADS + h, d=d, rows=rows, hc=hc, strict=strict, incl=incl, p_end=p_end[:, hc],
                    b_hat=(kk_h * a[:, hc] * e_inv[:, hc]).astype(BF16), k_hat=(kd[:, hc] * e_inv[:, hc]).astype(BF16),
                    kk_t=kk_h * e_exc[:, hc], r_t=r[:, hc] * e_inc[:, hc], v_h=v[:, hc].astype(BF16)))
                if not rev:
                    bonus_s[rows, hc] = jnp.sum(r[:, hc] * kd_sum[:, hc] * rk_ref[:, hc], axis=1, keepdims=True) * v[:, hc]

        for s in streams:
            s["p4"] = _dot_nt(jnp.concatenate([s["kk_t"], s["r_t"]], axis=0).astype(BF16),
                              jnp.concatenate([s["b_hat"], s["k_hat"]], axis=0))
            s["pw"] = [jnp.where(s["strict"], -s["p4"][:CHUNK, :CHUNK], 0.0).astype(BF16)]
            s["pair"] = []
        for step in range(1, 6):
            for s in streams:
                s["pw"].append(jnp.dot(s["pw"][-1], s["pw"][-1], preferred_element_type=F32).astype(BF16))
            if step % 2 == 1:
                for s in streams:
                    lo, hi = s["pw"][step - 1], s["pw"][step]
                    s["pair"].append(ident + lo.astype(F32) + hi.astype(F32) + jnp.dot(lo, hi, preferred_element_type=F32))
        for s in streams:
            s["inv"] = bdot(s["pair"][0], s["pair"][1])
        for s in streams:
            s["inv"] = bdot(s["inv"], s["pair"][2])
        for s in streams:
            p4 = s["p4"]
            masked = jnp.concatenate([jnp.where(s["strict"], p4[:CHUNK, CHUNK:], 0.0),
                                      jnp.where(s["incl"], p4[CHUNK:, CHUNK:], 0.0)], axis=0)
            s["akv_rkv"] = bdot(masked, s["v_h"])
        for s in streams:
            s["m12"] = bdot(s["inv"], jnp.concatenate([s["kk_t"], s["akv_rkv"][:CHUNK]], axis=1)).astype(BF16)
        for s in streams:
            s["gh"] = _dot_tn(s["m12"], s["b_hat"])
            s["vk"] = _dot_tn(s["v_h"], s["k_hat"])
        for s in streams:
            s["qy"] = bdot(jnp.where(s["incl"], s["p4"][CHUNK:, :CHUNK], 0.0), s["m12"])
        for s in streams:
            g_mat = (ident - s["gh"][:HEAD_DIM]) * s["p_end"]
            h_mat = (s["vk"] - s["gh"][HEAD_DIM:]) * s["p_end"]
            q_mat = s["r_t"] - s["qy"][:, :HEAD_DIM]
            y0 = s["akv_rkv"][CHUNK:] - s["qy"][:, HEAD_DIM:]
            s0 = st_s[s["j"]].astype(BF16)
            y_s[s["d"], s["rows"], s["hc"]] = _dot_nt(q_mat.astype(BF16), s0) + y0
            st_s[s["j"]] = bdot(s0, g_mat) + h_mat
        return carry

    lax.fori_loop(0, nch // RW_CHUNKS_PER_STEP, chunk_step, 0)

    head_mean = _head_mean_matrix()

    def finish(i, carry):
        rows = [pl.ds(pl.multiple_of((RW_CHUNKS_PER_STEP * i + u) * CHUNK, CHUNK), CHUNK) for u in range(RW_CHUNKS_PER_STEP)]
        ys = [y_s[0, r, :] + y_s[1, r, :] for r in rows]
        ycs = [y - _split_dot(y, head_mean) for y in ys]
        vrs = [_split_dot(yc * yc, head_mean) for yc in ycs]
        for r, yc, vr in zip(rows, ycs, vrs):
            tot = yc * lax.rsqrt(vr + RW_NORM_EPS) * ng_ref[...] + nb_ref[...]
            out_ref[0, r, :] = ((tot + bonus_s[r, :]) * gate_s[r, :]).astype(BF16)
        return carry

    lax.fori_loop(0, nch // RW_CHUNKS_PER_STEP, finish, 0)


def _rwkv(rkv, small, conv_w, w0, w_up, a0, a_up, g_up, k_k, k_a, r_k, norm_g, norm_b):
    bsz, s, _ = rkv.shape
    vec = lambda t: t.astype(F32).reshape(1, BRANCH_W)
    f = lambda t: t.astype(F32)
    return pl.pallas_call(
        _rwkv_body,
        out_shape=jax.ShapeDtypeStruct((bsz, s, BRANCH_W), BF16),
        grid=(bsz,),
        in_specs=[pl.BlockSpec((1, s, W_RKV), lambda b: (b, 0, 0)),
                  pl.BlockSpec((1, s, W_SMALL), lambda b: (b, 0, 0)),
                  _resident((3, 3 * BRANCH_W)), _resident((2, BRANCH_W)), _resident((2, RW_RANK, BRANCH_W)),
                  _resident((2, BRANCH_W)), _resident((2, RW_RANK, BRANCH_W)), _resident((2 * RW_RANK, BRANCH_W)),
                  _resident((1, BRANCH_W)), _resident((1, BRANCH_W)), _resident((1, BRANCH_W)),
                  _resident((1, BRANCH_W)), _resident((1, BRANCH_W))],
        out_specs=pl.BlockSpec((1, s, BRANCH_W), lambda b: (b, 0, 0)),
        scratch_shapes=[pltpu.VMEM((2, s, BRANCH_W), F32), pltpu.VMEM((s, BRANCH_W), F32), pltpu.VMEM((s, BRANCH_W), F32),
                        pltpu.VMEM((N_STREAMS, HEAD_DIM, HEAD_DIM), F32)],
        compiler_params=_params(("parallel",)),
        name="rwkv7",
    )(rkv, small, f(conv_w), f(w0), w_up.astype(BF16), f(a0), a_up.astype(BF16), g_up.astype(BF16), vec(k_k), vec(k_a),
      vec(r_k), vec(norm_g), vec(norm_b))


def _trunk(x, p, w_in_perm):
    bsz, s, d = x.shape
    n = bsz * s
    seq = lambda t: t.reshape(bsz, s, t.shape[-1])
    flat = lambda t: t.reshape(n, t.shape[-1])
    xf = _ln_rows(x.reshape(n, d), p["ln0_g"], p["ln0_b"])
    for l in range(DEPTH):
        na, mlqkv, mlo, rkv, small, merge = _in_proj(xf, w_in_perm[l])
        o_a = _na_attention(seq(na), _na_bias_table(p["na_rpb"][l], s // GRID_W))
        o_b = _mlstm(seq(mlqkv), seq(mlo), seq(small), p["ml_conv"][l], p["ml_gate_b"][l], p["ml_norm_g"][l])
        o_c = _rwkv(seq(rkv), seq(small), p["rw_conv"][l], p["rw_w0"][l], p["rw_w_up"][l], p["rw_a0"][l], p["rw_a_up"][l],
                    p["rw_g_up"][l], p["rw_k_k"][l], p["rw_k_a"][l], p["rw_r_k"][l], p["rw_norm_g"][l], p["rw_norm_b"][l])
        xf = _mix_ln(xf, flat(o_a), flat(o_b), flat(o_c), merge, p["w_br_a"][l], p["w_br_b"][l], p["w_br_c"][l],
                     p["w_out"][l], p["ln1_g"][l], p["ln1_b"][l])
        xf = _moe_ln(xf, p["w_router"], p["router_bias"], p["moe_w1"][l], p["moe_w3"][l], p["moe_w2"][l],
                     p["ln2_g"][l], p["ln2_b"][l])
    return xf.reshape(bsz, s, d)


def kernel(x_prompt, x_sample, ln0_g, ln0_b, w_in, na_rpb, ml_conv, ml_gate_b, ml_norm_g, rw_conv, rw_w0, rw_w_up,
           rw_a0, rw_a_up, rw_g_up, rw_k_k, rw_k_a, rw_r_k, rw_norm_g, rw_norm_b, w_br_a, w_br_b, w_br_c, w_out,
           ln1_g, ln1_b, w_router, router_bias, moe_w1, moe_w3, moe_w2, ln2_g, ln2_b):
    p = {
        "ln0_g": ln0_g, "ln0_b": ln0_b, "na_rpb": na_rpb, "ml_conv": ml_conv,
        "ml_gate_b": ml_gate_b, "ml_norm_g": ml_norm_g, "rw_conv": rw_conv, "rw_w0": rw_w0,
        "rw_w_up": rw_w_up, "rw_a0": rw_a0, "rw_a_up": rw_a_up, "rw_g_up": rw_g_up, "rw_k_k": rw_k_k,
        "rw_k_a": rw_k_a, "rw_r_k": rw_r_k, "rw_norm_g": rw_norm_g, "rw_norm_b": rw_norm_b,
        "w_br_a": w_br_a, "w_br_b": w_br_b, "w_br_c": w_br_c, "w_out": w_out, "ln1_g": ln1_g,
        "ln1_b": ln1_b, "w_router": w_router, "router_bias": router_bias, "moe_w1": moe_w1,
        "moe_w3": moe_w3, "moe_w2": moe_w2, "ln2_g": ln2_g, "ln2_b": ln2_b,
    }
    w_in_perm = [_permute_w_in(w_in[l]) for l in range(DEPTH)]
    return _trunk(x_prompt, p, w_in_perm), _trunk(x_sample, p, w_in_perm)
```

```python
import functools

import jax
import jax.numpy as jnp
import numpy as np
from jax import lax
from jax.experimental import pallas as pl
from jax.experimental.pallas import tpu as pltpu

F32 = jnp.float32
BF16 = jnp.bfloat16

D_MODEL = 1024
DEPTH = 4
GRID_W = 64
HEAD_DIM = 64
N_HEADS = 4
BRANCH_W = N_HEADS * HEAD_DIM
CHUNK = 64
NA_WIN_ROWS = 8
NA_WIN_COLS = 16
NA_ROW_CLASSES = 8
ML_NORM_EPS = 1e-6
RW_RANK = 32
RW_DECAY_SCALE = 0.606531
RW_NORM_EPS = 64e-5
N_EXPERTS = 16
EXPERTS_PER_GROUP = 4
D_EXPERT = 512
ALPHA = (2 * DEPTH) ** 0.25
LN_EPS = 1e-5
NEG = -1e30

W_NA = 3 * BRANCH_W
W_MLQKV = 3 * BRANCH_W
W_MLO = BRANCH_W
W_RKV = 3 * BRANCH_W
W_SMALL = 256
W_MERGE = 3 * D_MODEL
PIECES = (W_NA, W_MLQKV, W_MLO, W_RKV, W_SMALL, W_MERGE)
D_INP = sum(PIECES)
SMALL_GATE_OFF = 192

VMEM_LIMIT = 56 * 1024 * 1024
TOKEN_TILE = 1024


def _params(sem):
    return pltpu.CompilerParams(dimension_semantics=sem, vmem_limit_bytes=VMEM_LIMIT)


def _resident(shape):
    nd = len(shape)
    return pl.BlockSpec(shape, lambda *_: (0,) * nd, pipeline_mode=pl.Buffered(1))


def _ln(z, g, b):
    mu = jnp.mean(z, axis=-1, keepdims=True)
    zc = z - mu
    var = jnp.mean(zc * zc, axis=-1, keepdims=True)
    return zc * lax.rsqrt(var + LN_EPS) * g + b


def _ln_rows_body(x_ref, g_ref, b_ref, o_ref):
    o_ref[...] = _ln(x_ref[...], g_ref[...], b_ref[...])


def _ln_rows(x, g, b):
    n, d = x.shape
    tm = TOKEN_TILE
    return pl.pallas_call(
        _ln_rows_body,
        out_shape=jax.ShapeDtypeStruct((n, d), F32),
        grid=(n // tm,),
        in_specs=[pl.BlockSpec((tm, d), lambda i: (i, 0)), _resident((1, d)), _resident((1, d))],
        out_specs=pl.BlockSpec((tm, d), lambda i: (i, 0)),
        compiler_params=_params(("parallel",)),
        name="ln_rows",
    )(x, g.reshape(1, d), b.reshape(1, d))


def _in_proj_body(x_ref, w_ref, *o_refs):
    xb = x_ref[...].astype(BF16)
    off = 0
    for o_ref, width in zip(o_refs, PIECES):
        for c0 in range(0, width, 768):
            c1 = min(c0 + 768, width)
            o_ref[:, c0:c1] = jnp.dot(xb, w_ref[:, off + c0:off + c1], preferred_element_type=F32).astype(BF16)
        off += width


def _in_proj(x, w):
    n, d = x.shape
    tm = TOKEN_TILE
    return pl.pallas_call(
        _in_proj_body,
        out_shape=[jax.ShapeDtypeStruct((n, wd), BF16) for wd in PIECES],
        grid=(n // tm,),
        in_specs=[pl.BlockSpec((tm, d), lambda i: (i, 0)), _resident((d, D_INP))],
        out_specs=[pl.BlockSpec((tm, wd), lambda i: (i, 0)) for wd in PIECES],
        compiler_params=_params(("parallel",)),
        name="in_proj",
    )(x, w)


def _permute_w_in(w_in_l):
    na, mlqkv, mlo, mlg, rkv, wl, al, gl, merge = jnp.split(
        w_in_l, np.cumsum([768, 768, 256, 16, 768, 64, 64, 64])[:8].tolist(), axis=1)
    pad = jnp.zeros((w_in_l.shape[0], W_SMALL - 208), w_in_l.dtype)
    return jnp.concatenate([na, mlqkv, mlo, rkv, wl, al, gl, mlg, pad, merge], axis=1).astype(BF16)


def _mix_ln_body(x_ref, oa_ref, ob_ref, oc_ref, mg_ref, wa_ref, wb_ref, wc_ref, wo_ref, g_ref, b_ref, o_ref):
    mixed = None
    for j, (o_br, w_br) in enumerate(((oa_ref, wa_ref), (ob_ref, wb_ref), (oc_ref, wc_ref))):
        gate = jax.nn.sigmoid(mg_ref[:, j * D_MODEL:(j + 1) * D_MODEL].astype(F32))
        term = gate * jnp.dot(o_br[...], w_br[...], preferred_element_type=F32)
        mixed = term if mixed is None else mixed + term
    z = ALPHA * x_ref[...] + jnp.dot(mixed.astype(BF16), wo_ref[...], preferred_element_type=F32)
    o_ref[...] = _ln(z, g_ref[...], b_ref[...])


def _mix_ln(x, oa, ob, oc, merge, wa, wb, wc, wo, g, b):
    n, d = x.shape
    tm = TOKEN_TILE
    row = lambda wd: pl.BlockSpec((tm, wd), lambda i: (i, 0))
    return pl.pallas_call(
        _mix_ln_body,
        out_shape=jax.ShapeDtypeStruct((n, d), F32),
        grid=(n // tm,),
        in_specs=[row(d), row(BRANCH_W), row(BRANCH_W), row(BRANCH_W), row(W_MERGE),
                  _resident((BRANCH_W, d)), _resident((BRANCH_W, d)), _resident((BRANCH_W, d)), _resident((d, d)),
                  _resident((1, d)), _resident((1, d))],
        out_specs=row(d),
        compiler_params=_params(("parallel",)),
        name="mix_ln",
    )(x, oa, ob, oc, merge, wa.astype(BF16), wb.astype(BF16), wc.astype(BF16), wo.astype(BF16),
      g.reshape(1, d), b.reshape(1, d))


MOE_TILE = 1024


def _top2_sum(a, b, c, d):
    hi1, lo1 = jnp.maximum(a, b), jnp.minimum(a, b)
    hi2, lo2 = jnp.maximum(c, d), jnp.minimum(c, d)
    return jnp.maximum(hi1, hi2) + jnp.maximum(jnp.minimum(hi1, hi2), jnp.maximum(lo1, lo2))


def _router_gates(x, wrt, rbias):
    w_hi, x_hi = wrt.astype(BF16), x.astype(BF16)
    w_lo, x_lo = (wrt - w_hi.astype(F32)).astype(BF16), (x - x_hi.astype(F32)).astype(BF16)
    logits = _dot_nt(w_hi, x_hi) + (_dot_nt(w_hi, x_lo) + _dot_nt(w_lo, x_hi))
    ex = jnp.exp(logits - jnp.max(logits, axis=0, keepdims=True))
    probs = ex / jnp.sum(ex, axis=0, keepdims=True)
    sel = probs + rbias
    rows = [sel[e:e + 1, :] for e in range(N_EXPERTS)]
    n_groups = N_EXPERTS // EXPERTS_PER_GROUP
    scores = [_top2_sum(*rows[EXPERTS_PER_GROUP * g:EXPERTS_PER_GROUP * (g + 1)]) for g in range(n_groups)]
    best, gidx = scores[0], jnp.zeros_like(scores[0], dtype=jnp.int32)
    for g in range(1, n_groups):
        better = scores[g] > best
        gidx = jnp.where(better, g, gidx)
        best = jnp.where(better, scores[g], best)
    picked = []
    for e in range(N_EXPERTS):
        g = e // EXPERTS_PER_GROUP
        rank = jnp.zeros_like(gidx)
        for e2 in range(EXPERTS_PER_GROUP * g, EXPERTS_PER_GROUP * (g + 1)):
            if e2 == e:
                continue
            ahead = (rows[e2] >= rows[e]) if e2 < e else (rows[e2] > rows[e])
            rank = rank + ahead.astype(jnp.int32)
        chosen = (gidx == g) & (rank < 2)
        picked.append(jnp.where(chosen, probs[e:e + 1, :], 0.0))
    total = picked[0]
    for p in picked[1:]:
        total = total + p
    slots = []
    for k in range(EXPERTS_PER_GROUP):
        acc = picked[k]
        for g in range(1, n_groups):
            acc = acc + picked[EXPERTS_PER_GROUP * g + k]
        slots.append(acc / total)
    return jnp.concatenate(slots, axis=0), gidx


def _moe_ln_body(x_ref, wrt_ref, rb_ref, tri_ref, w1_ref, w3_ref, w2_ref, g_ref, b_ref, o_ref,
                 xb_s, rank_s, gidx_s, slot_s, pick_s, ys_s, cnt_s):
    grp = pl.program_id(1)
    tm = x_ref.shape[0]
    n_groups = N_EXPERTS // EXPERTS_PER_GROUP

    @pl.when(grp == 0)
    def _():
        x = x_ref[...]
        xb_s[...] = x.astype(BF16)
        slots, gidx = _router_gates(x, wrt_ref[...], rb_ref[...])
        hi = slots.astype(BF16)
        slot_s[...] = jnp.concatenate([hi, (slots - hi.astype(F32)).astype(BF16)], axis=0)
        gidx_s[...] = gidx
        member = [jnp.where(gidx == g, 1.0, 0.0) for g in range(n_groups)]
        onehot = jnp.concatenate(member + [jnp.zeros((8 - n_groups, tm), F32)], axis=0)
        upto = jnp.dot(onehot.astype(BF16), tri_ref[...], preferred_element_type=F32)
        rank = -1.0
        for g in range(n_groups):
            rank = rank + member[g] * upto[g:g + 1, :]
            cnt_s[g] = jnp.sum(member[g]).astype(jnp.int32)
        rank_s[...] = rank
        o_ref[...] = jnp.zeros_like(o_ref)

    n_blocks = (cnt_s[grp] + MOE_BLOCK - 1) // MOE_BLOCK

    def block(blk):
        want = (lax.broadcasted_iota(jnp.int32, (MOE_BLOCK, tm), 0) + blk * MOE_BLOCK).astype(F32)
        pick = jnp.where((rank_s[...] == want) & (gidx_s[...] == grp), 1.0, 0.0).astype(BF16)
        xs = jnp.dot(pick, xb_s[...], preferred_element_type=F32).astype(BF16)
        gates = _dot_nt(pick, slot_s[...])
        gates = gates[:, :EXPERTS_PER_GROUP] + gates[:, EXPERTS_PER_GROUP:]
        ys = None
        for pair in range(0, EXPERTS_PER_GROUP, 2):
            ks = (pair, pair + 1)
            h1s = [jnp.dot(xs, w1_ref[k], preferred_element_type=F32) for k in ks]
            h3s = [jnp.dot(xs, w3_ref[k], preferred_element_type=F32) for k in ks]
            hs = [(h1 * jax.nn.sigmoid(h1) * h3 * gates[:, k:k + 1]).astype(BF16) for k, h1, h3 in zip(ks, h1s, h3s)]
            for k, h in zip(ks, hs):
                y = jnp.dot(h, w2_ref[k], preferred_element_type=F32)
                ys = y if ys is None else ys + y
        return pick, ys.astype(BF16)

    first = pl.ds(pl.multiple_of(grp * MOE_BLOCK, MOE_BLOCK), MOE_BLOCK)
    pick_s[first, :], ys_s[first, :] = block(0)

    def extra(blk, carry):
        pick, ys = block(blk)
        o_ref[...] += _dot_tn(pick, ys)
        return carry

    lax.fori_loop(1, n_blocks, extra, 0)

    @pl.when(grp == n_groups - 1)
    def _():
        y = o_ref[...] + _dot_tn(pick_s[...], ys_s[...])
        o_ref[...] = _ln(ALPHA * x_ref[...] + y, g_ref[...], b_ref[...])


MOE_BLOCK = 320


def _moe_ln(x, w_router, router_bias, w1, w3, w2, g, b):
    n, d = x.shape
    tm = MOE_TILE
    tri = (np.arange(tm)[:, None] <= np.arange(tm)[None, :])
    grp_w = lambda shape: pl.BlockSpec((EXPERTS_PER_GROUP,) + shape, lambda i, g: (g, 0, 0))
    return pl.pallas_call(
        _moe_ln_body,
        out_shape=jax.ShapeDtypeStruct((n, d), F32),
        grid=(n // tm, N_EXPERTS // EXPERTS_PER_GROUP),
        in_specs=[pl.BlockSpec((tm, d), lambda i, g: (i, 0)),
                  _resident((N_EXPERTS, d)), _resident((N_EXPERTS, 1)), _resident((tm, tm)),
                  grp_w((d, D_EXPERT)), grp_w((d, D_EXPERT)), grp_w((D_EXPERT, d)),
                  _resident((1, d)), _resident((1, d))],
        out_specs=pl.BlockSpec((tm, d), lambda i, g: (i, 0)),
        scratch_shapes=[pltpu.VMEM((tm, d), BF16), pltpu.VMEM((1, tm), F32), pltpu.VMEM((1, tm), jnp.int32),
                        pltpu.VMEM((2 * EXPERTS_PER_GROUP, tm), BF16),
                        pltpu.VMEM((N_EXPERTS // EXPERTS_PER_GROUP * MOE_BLOCK, tm), BF16),
                        pltpu.VMEM((N_EXPERTS // EXPERTS_PER_GROUP * MOE_BLOCK, d), BF16),
                        pltpu.SMEM((N_EXPERTS // EXPERTS_PER_GROUP,), jnp.int32)],
        compiler_params=_params(("parallel", "arbitrary")),
        name="moe_ln",
    )(x, w_router.T, router_bias.reshape(N_EXPERTS, 1), jnp.asarray(tri, BF16), w1.astype(BF16), w3.astype(BF16),
      w2.astype(BF16), g.reshape(1, d), b.reshape(1, d))


def _na_bias_table(rpb, rows):
    assert rows >= NA_WIN_ROWS
    cols = np.arange(GRID_W)
    win = np.clip(cols - NA_WIN_COLS // 2, 0, GRID_W - NA_WIN_COLS)
    rel = cols[None, :] - cols[:, None]
    ok = (cols[None, :] >= win[:, None]) & (cols[None, :] < win[:, None] + NA_WIN_COLS)
    pick = (np.arange(2 * NA_WIN_COLS - 1)[None, None, :] == (rel + NA_WIN_COLS - 1)[:, :, None]) & ok[:, :, None]
    t = jnp.einsum("hdr,ckr->hdck", rpb.astype(F32), jnp.asarray(pick, F32), precision=lax.Precision.HIGHEST)
    t = jnp.where(ok[None, None], t, NEG)
    per_class = [t[:, NA_WIN_ROWS - 1 - cls:2 * NA_WIN_ROWS - 1 - cls] for cls in range(NA_ROW_CLASSES)]
    t = jnp.stack(per_class, axis=1)
    return jnp.transpose(t, (0, 1, 3, 2, 4)).reshape(rpb.shape[0], NA_ROW_CLASSES, GRID_W, NA_WIN_ROWS * GRID_W)


NA_ROWS_PER_STEP = 2


def _na_body(qkv_ref, bias_ref, o_ref, vx_s):
    rows = qkv_ref.shape[1] // GRID_W
    band = NA_WIN_ROWS * GRID_W

    head_cols = lambda base, h: slice(base + h * HEAD_DIM, base + (h + 1) * HEAD_DIM)
    ones_col = jnp.where(lax.broadcasted_iota(jnp.int32, (GRID_W, HEAD_DIM), 1) == 0, 1.0, 0.0).astype(BF16)

    def extend(r, carry):
        q_rows = pl.ds(pl.multiple_of(r * GRID_W, GRID_W), GRID_W)
        for h in range(N_HEADS):
            vx_s[q_rows, 2 * h * HEAD_DIM:(2 * h + 1) * HEAD_DIM] = qkv_ref[0, q_rows, head_cols(2 * BRANCH_W, h)]
            vx_s[q_rows, (2 * h + 1) * HEAD_DIM:(2 * h + 2) * HEAD_DIM] = ones_col
        return carry

    lax.fori_loop(0, rows, extend, 0)

    def row_step(i, carry):
        units = []
        for u in range(NA_ROWS_PER_STEP):
            r = NA_ROWS_PER_STEP * i + u
            rs = jnp.clip(r - NA_WIN_ROWS // 2, 0, rows - NA_WIN_ROWS)
            cls = jnp.where(r < NA_WIN_ROWS // 2, r, NA_WIN_ROWS // 2 + jnp.maximum(r - (rows - NA_WIN_ROWS // 2), 0))
            q_rows = pl.ds(pl.multiple_of(r * GRID_W, GRID_W), GRID_W)
            k_rows = pl.ds(pl.multiple_of(rs * GRID_W, GRID_W), band)
            units += [(h, cls, q_rows, k_rows) for h in range(N_HEADS)]
        scores = [_dot_nt(qkv_ref[0, q_rows, head_cols(0, h)], qkv_ref[0, k_rows, head_cols(BRANCH_W, h)])
                  for h, _, q_rows, k_rows in units]
        maxes = [jnp.max(scores[n] * (HEAD_DIM ** -0.5) + bias_ref[h, cls], axis=1, keepdims=True)
                 for n, (h, cls, _, _) in enumerate(units)]
        probs = [jnp.exp(scores[n] * (HEAD_DIM ** -0.5) + bias_ref[h, cls] - maxes[n]) for n, (h, cls, _, _) in enumerate(units)]
        outs = [jnp.dot(probs[n].astype(BF16), vx_s[k_rows, 2 * h * HEAD_DIM:(2 * h + 2) * HEAD_DIM], preferred_element_type=F32)
                for n, (h, _, _, k_rows) in enumerate(units)]
        for n, (h, _, q_rows, _) in enumerate(units):
            o_ref[0, q_rows, head_cols(0, h)] = (outs[n][:, :HEAD_DIM] / outs[n][:, HEAD_DIM:HEAD_DIM + 1]).astype(BF16)
        return carry

    lax.fori_loop(0, rows // NA_ROWS_PER_STEP, row_step, 0)


def _na_attention(qkv, bias):
    bsz, s, _ = qkv.shape
    return pl.pallas_call(
        _na_body,
        out_shape=jax.ShapeDtypeStruct((bsz, s, BRANCH_W), BF16),
        grid=(bsz,),
        in_specs=[pl.BlockSpec((1, s, W_NA), lambda b: (b, 0, 0)), _resident(bias.shape)],
        out_specs=pl.BlockSpec((1, s, BRANCH_W), lambda b: (b, 0, 0)),
        scratch_shapes=[pltpu.VMEM((s, 2 * BRANCH_W), BF16)],
        compiler_params=_params(("parallel",)),
        name="na_attention",
    )(qkv, bias)


def _eye(n):
    return lax.broadcasted_iota(jnp.int32, (n, n), 0) == lax.broadcasted_iota(jnp.int32, (n, n), 1)


def _row_to_col(row, eye):
    return jnp.sum(jnp.where(eye, row, 0.0), axis=1, keepdims=True)


def _conv3_rows(ref, c, nch, cols, w_ref):
    seq = nch * CHUNK
    r0 = pl.multiple_of(c * CHUNK, CHUNK)
    x = ref[0, pl.ds(r0, CHUNK), cols].astype(F32)
    lo = pl.multiple_of(jnp.maximum(r0 - 16, 0), 16)
    hi = pl.multiple_of(jnp.minimum(r0 + CHUNK, seq - 16), 16)
    prev = ref[0, pl.ds(lo, 16), cols][15:16, :].astype(F32) * jnp.where(c > 0, 1.0, 0.0)
    nxt = ref[0, pl.ds(hi, 16), cols][0:1, :].astype(F32) * jnp.where(c < nch - 1, 1.0, 0.0)
    rid = lax.broadcasted_iota(jnp.int32, x.shape, 0)
    xm = jnp.where(rid == 0, prev, pltpu.roll(x, 1, axis=0))
    xp = jnp.where(rid == CHUNK - 1, nxt, pltpu.roll(x, CHUNK - 1, axis=0))
    return w_ref[0:1, :] * xm + w_ref[1:2, :] * x + w_ref[2:3, :] * xp


def _head_norm(h, eps):
    mu = jnp.mean(h, axis=-1, keepdims=True)
    hc = h - mu
    return hc * lax.rsqrt(jnp.mean(hc * hc, axis=-1, keepdims=True) + eps)


def _head_mean_matrix():
    hi = lax.broadcasted_iota(jnp.int32, (BRANCH_W, BRANCH_W), 0) // HEAD_DIM
    hj = lax.broadcasted_iota(jnp.int32, (BRANCH_W, BRANCH_W), 1) // HEAD_DIM
    return jnp.where(hi == hj, 1.0 / HEAD_DIM, 0.0).astype(BF16)


def _split_dot(x, m):
    hi = x.astype(BF16)
    lo = (x - hi.astype(F32)).astype(BF16)
    return jnp.dot(hi, m, preferred_element_type=F32) + jnp.dot(lo, m, preferred_element_type=F32)


def _head_norm_wide(y, head_mean, eps):
    yc = y - _split_dot(y, head_mean)
    return yc * lax.rsqrt(_split_dot(yc * yc, head_mean) + eps)


def _dot_nt(a, b, **kw):
    return lax.dot_general(a, b, (((1,), (1,)), ((), ())), preferred_element_type=F32, **kw)


def _dot_tn(a, b, **kw):
    return lax.dot_general(a, b, (((0,), (0,)), ((), ())), preferred_element_type=F32, **kw)


N_STREAMS = 2 * N_HEADS


ML_CHUNKS_PER_STEP = 4
STATE_W = 2 * HEAD_DIM


def _mlstm_body(qkv_ref, o_ref_in, g_ref, gb_ref, cw_ref, ng_ref, out_ref,
                qk_s, vx_s, ig_s, b_s, ct_s, mc_s, bl_s, mprev_s, cst_s, mst_s):
    seq = qkv_ref.shape[1]
    nch = seq // CHUNK
    eye = _eye(CHUNK)
    ri = lax.broadcasted_iota(jnp.int32, (CHUNK, CHUNK), 0)
    ci = lax.broadcasted_iota(jnp.int32, (CHUNK, CHUNK), 1)
    head_cols = lambda base, h: slice(base + h * HEAD_DIM, base + (h + 1) * HEAD_DIM)
    head_mean = _head_mean_matrix()
    ones_col =jnp.where(lax.broadcasted_iota(jnp.int32, (CHUNK, HEAD_DIM), 1) == 0, 1.0, 0.0).astype(BF16)

    def conv_step(c, carry):
        rows = pl.ds(pl.multiple_of(c * CHUNK, CHUNK), CHUNK)
        y = _conv3_rows(qkv_ref, c, nch, slice(0, 2 * BRANCH_W), cw_ref)
        y = y * jax.nn.sigmoid(y)
        scale = jnp.where(lax.broadcasted_iota(jnp.int32, (1, 2 * BRANCH_W), 1) < BRANCH_W, 1.0, HEAD_DIM ** -0.5)
        qk_s[rows, :] = (y * scale).astype(BF16)
        for h in range(N_HEADS):
            vx_s[rows, h * STATE_W:h * STATE_W + HEAD_DIM] = qkv_ref[0, rows, head_cols(2 * BRANCH_W, h)]
            vx_s[rows, h * STATE_W + HEAD_DIM:(h + 1) * STATE_W] = ones_col
        return carry

    lax.fori_loop(0, nch, conv_step, 0)

    for j in range(N_STREAMS):
        rev, h = j >= N_HEADS, j % N_HEADS
        gi = (2 * N_HEADS if rev else 0) + h
        ig_s[j] = g_ref[0, gi] + gb_ref[gi]
        fpre = g_ref[0, gi + N_HEADS] + gb_ref[gi + N_HEADS]
        lf = jnp.minimum(fpre, 0.0) - jnp.log1p(jnp.exp(-jnp.abs(fpre)))
        cum = (ri >= ci) if rev else (ri <= ci)
        b_s[j] = jnp.dot(lf, cum.astype(F32), precision=lax.Precision.HIGHEST, preferred_element_type=F32)
        cst_s[j] = jnp.zeros((HEAD_DIM, STATE_W), F32)
        mst_s[j] = jnp.full((1, 128), NEG, F32)

    def chunk_stats(i, carry):
        chunks = [ML_CHUNKS_PER_STEP * i + u for u in range(ML_CHUNKS_PER_STEP)]
        rows = [pl.ds(pl.multiple_of(c * CHUNK, CHUNK), CHUNK) for c in chunks]
        units = [(chunks[u], rows[u], j) for u in range(ML_CHUNKS_PER_STEP) for j in range(N_STREAMS)]
        a_rs, mcs = [], []
        for c, _, j in units:
            b_r = b_s[j, pl.ds(c, 1), :]
            bl = b_r[:, 0:1] if j >= N_HEADS else b_r[:, CHUNK - 1:CHUNK]
            a_rs.append(bl - b_r + ig_s[j, pl.ds(c, 1), :])
            bl_s[j, c] = jnp.broadcast_to(bl, (1, 128))
        for n, (c, _, j) in enumerate(units):
            mcs.append(jnp.max(a_rs[n], axis=1, keepdims=True))
            mc_s[j, c] = jnp.broadcast_to(mcs[n], (1, 128))
        diags = [jnp.where(eye, jnp.exp(a_rs[n] - mcs[n]), 0.0).astype(BF16) for n in range(len(units))]
        wvs = [jnp.dot(diags[n], vx_s[r, (j % N_HEADS) * STATE_W:(j % N_HEADS + 1) * STATE_W],
                       preferred_element_type=F32).astype(BF16) for n, (_, r, j) in enumerate(units)]
        for n, (c, r, j) in enumerate(units):
            ct_s[j, c] = _dot_tn(qk_s[r, head_cols(BRANCH_W, j % N_HEADS)], wvs[n])
        return carry

    lax.fori_loop(0, nch // ML_CHUNKS_PER_STEP, chunk_stats, 0)

    def scan_step(i, carry):
        for j in range(N_STREAMS):
            c = (nch - 1 - i) if j >= N_HEADS else i
            ct, m = cst_s[j], mst_s[j]
            bl, mc = bl_s[j, c], mc_s[j, c]
            m_new = jnp.maximum(bl + m, mc)
            cst_s[j] = jnp.exp(bl + m - m_new) * ct + jnp.exp(mc - m_new) * ct_s[j, c]
            mst_s[j] = m_new
            ct_s[j, c] = ct
            mprev_s[j, c] = m
        return carry

    lax.fori_loop(0, nch, scan_step, 0)

    sub = lax.broadcasted_iota(jnp.int32, (16, CHUNK), 0)
    sub_w = lax.broadcasted_iota(jnp.int32, (16, 2 * CHUNK), 0)
    lane_w = lax.broadcasted_iota(jnp.int32, (16, 2 * CHUNK), 1)
    first_lane = jnp.where(lax.broadcasted_iota(jnp.int32, (1, CHUNK), 1) == 0, 1.0, 0.0)
    rhs_fixed = jnp.where((sub_w < 2) & (lane_w < CHUNK + 2), 1.0,
                          jnp.where((sub_w >= 4) & (sub_w < 6) & (lane_w == CHUNK + 1), -1.0, 0.0))
    pad = jnp.full((1, CHUNK), NEG, F32)

    def hi_lo(x):
        hi = x.astype(BF16).astype(F32)
        return hi, x - hi

    def chunk_out(i, carry):
        chunks = [ML_CHUNKS_PER_STEP * i + u for u in range(ML_CHUNKS_PER_STEP)]
        rows = [pl.ds(pl.multiple_of(c * CHUNK, CHUNK), CHUNK) for c in chunks]
        units = [(u, j) for u in range(ML_CHUNKS_PER_STEP) for j in range(N_STREAMS)]
        qs = [[qk_s[r, head_cols(0, h)] for h in range(N_HEADS)] for r in rows]
        vs = [[vx_s[r, h * STATE_W:(h + 1) * STATE_W] for h in range(N_HEADS)] for r in rows]
        qks = [[_dot_nt(qs[u][h], qk_s[r, head_cols(BRANCH_W, h)]) for h in range(N_HEADS)] for u, r in enumerate(rows)]
        inter_parts = [jnp.dot(qs[u][j % N_HEADS], ct_s[j, chunks[u]].astype(BF16), preferred_element_type=F32)
                       for u, j in units]
        b_rs = [b_s[j, pl.ds(chunks[u], 1), :] for u, j in units]
        d_rs = [ig_s[j, pl.ds(chunks[u], 1), :] - b_rs[n] for n, (u, j) in enumerate(units)]
        m_prevs = [mprev_s[j, chunks[u]][:, 0:1] for u, j in units]
        run = jnp.concatenate([jnp.concatenate([d, pad], axis=1) for d in d_rs], axis=0)
        fwd_rows = lax.broadcasted_iota(jnp.int32, run.shape, 0) % N_STREAMS < N_HEADS
        shift = 1
        while shift < CHUNK:
            moved = jnp.where(fwd_rows, pltpu.roll(run, shift, axis=1), pltpu.roll(run, 2 * CHUNK - shift, axis=1))
            run = jnp.maximum(run, moved)
            shift *= 2
        exps = []
        for n in range(len(units)):
            u_hi, u_lo = hi_lo(-jnp.maximum(m_prevs[n], run[n:n + 1, :CHUNK]))
            b_hi, b_lo = hi_lo(b_rs[n])
            mid_hi, mid_lo = hi_lo(jnp.concatenate([d_rs[n], m_prevs[n] * first_lane], axis=1))
            lhs = jnp.where(sub == 0, u_hi, jnp.where(sub == 1, u_lo, jnp.where((sub == 2) | (sub == 3), 1.0,
                            jnp.where(sub == 4, b_hi, jnp.where(sub == 5, b_lo, 0.0)))))
            rhs = jnp.where(sub_w == 2, mid_hi, jnp.where(sub_w == 3, mid_lo, rhs_fixed))
            exps.append(jnp.exp(_dot_tn(lhs.astype(BF16), rhs.astype(BF16))))
        w_intras = []
        for n, (u, j) in enumerate(units):
            tri = (ci >= ri) if j >= N_HEADS else (ci <= ri)
            w_intras.append((jnp.where(tri, exps[n][:, :CHUNK], 0.0) * qks[u][j % N_HEADS]).astype(BF16))
        intra_parts = [jnp.dot(w_intras[n], vs[u][j % N_HEADS], preferred_element_type=F32) for n, (u, j) in enumerate(units)]
        hs = []
        for n in range(len(units)):
            inter, clamp = exps[n][:, CHUNK:CHUNK + 1], exps[n][:, CHUNK + 1:CHUNK + 2]
            both = inter * inter_parts[n] + intra_parts[n]
            den = both[:, HEAD_DIM:HEAD_DIM + 1]
            hs.append(both[:, :HEAD_DIM] / jnp.maximum(jnp.abs(den), clamp))
        tots = [jnp.concatenate([hs[u * N_STREAMS + h] + hs[u * N_STREAMS + N_HEADS + h] for h in range(N_HEADS)], axis=1)
                for u in range(ML_CHUNKS_PER_STEP)]
        centred = [t - _split_dot(t, head_mean) for t in tots]
        variances = [_split_dot(t * t, head_mean) for t in centred]
        for u in range(ML_CHUNKS_PER_STEP):
            tot = centred[u] * lax.rsqrt(variances[u] + ML_NORM_EPS) * ng_ref[...]
            out_ref[0, rows[u], :] = (tot * jax.nn.sigmoid(o_ref_in[0, rows[u], :].astype(F32))).astype(BF16)
        return carry

    lax.fori_loop(0, nch // ML_CHUNKS_PER_STEP, chunk_out, 0)


def _mlstm(qkv, o_pre, small, conv_w, gate_b, norm_g):
    bsz, s, _ = qkv.shape
    nch = s // CHUNK
    n_gate = 4 * N_HEADS
    gates = small[..., SMALL_GATE_OFF:SMALL_GATE_OFF + n_gate].astype(F32)
    gates = jnp.transpose(gates, (0, 2, 1)).reshape(bsz, n_gate, nch, CHUNK)
    gate_b = jnp.broadcast_to(gate_b.astype(F32).reshape(n_gate, 1, 1), (n_gate, 1, CHUNK))
    row = lambda shape: pltpu.VMEM(shape, F32)
    return pl.pallas_call(
        _mlstm_body,
        out_shape=jax.ShapeDtypeStruct((bsz, s, BRANCH_W), BF16),
        grid=(bsz,),
        in_specs=[pl.BlockSpec((1, s, W_MLQKV), lambda b: (b, 0, 0)),
                  pl.BlockSpec((1, s, BRANCH_W), lambda b: (b, 0, 0)),
                  pl.BlockSpec((1, n_gate, nch, CHUNK), lambda b: (b, 0, 0, 0)),
                  _resident((n_gate, 1, CHUNK)), _resident((3, 2 * BRANCH_W)), _resident((1, BRANCH_W))],
        out_specs=pl.BlockSpec((1, s, BRANCH_W), lambda b: (b, 0, 0)),
        scratch_shapes=[pltpu.VMEM((s, 2 * BRANCH_W), BF16), pltpu.VMEM((s, N_HEADS * STATE_W), BF16),
                        row((N_STREAMS, nch, CHUNK)), row((N_STREAMS, nch, CHUNK)),
                        row((N_STREAMS, nch, HEAD_DIM, STATE_W)),
                        row((N_STREAMS, nch, 1, 128)), row((N_STREAMS, nch, 1, 128)), row((N_STREAMS, nch, 1, 128)),
                        row((N_STREAMS, HEAD_DIM, STATE_W)), row((N_STREAMS, 1, 128))],
        compiler_params=_params(("parallel",)),
        name="mlstm",
    )(qkv, o_pre, gates, gate_b, conv_w.astype(F32), norm_g.astype(F32).reshape(1, BRANCH_W))


RW_CHUNKS_PER_STEP = 4


def _rwkv_body(rkv_ref, sm_ref, cw_ref, w0_ref, wup_ref, a0_ref, aup_ref, gup_ref, kk_ref, ka_ref, rk_ref, ng_ref, nb_ref,
               out_ref, y_s, bonus_s, gate_s, st_s):
    seq = rkv_ref.shape[1]
    nch = seq // CHUNK
    ri = lax.broadcasted_iota(jnp.int32, (CHUNK, CHUNK), 0)
    ci = lax.broadcasted_iota(jnp.int32, (CHUNK, CHUNK), 1)
    ident = (ri == ci).astype(F32)

    bdot = lambda x, y: jnp.dot(x.astype(BF16), y.astype(BF16), preferred_element_type=F32)

    def low_rank(x, d, up_ref, bias_ref):
        return bias_ref[d:d + 1, :] + bdot(x[:, d * RW_RANK:(d + 1) * RW_RANK], up_ref[d])

    st_s[...] = jnp.zeros_like(st_s)

    def chunk_step(i, carry):
        streams = []
        for rev, u in [(rev, u) for rev in (False, True) for u in range(RW_CHUNKS_PER_STEP)]:
            d = 1 if rev else 0
            strict = (ci > ri) if rev else (ci < ri)
            incl = (ci >= ri) if rev else (ci <= ri)
            last = slice(0, 1) if rev else slice(CHUNK - 1, CHUNK)
            c = (nch - 1 - (RW_CHUNKS_PER_STEP * i + u)) if rev else (RW_CHUNKS_PER_STEP * i + u)
            rows = pl.ds(pl.multiple_of(c * CHUNK, CHUNK), CHUNK)
            rkv = _conv3_rows(rkv_ref, c, nch, slice(0, 3 * BRANCH_W), cw_ref)
            r, k, v = rkv[:, :BRANCH_W], rkv[:, BRANCH_W:2 * BRANCH_W], rkv[:, 2 * BRANCH_W:]
            sm = sm_ref[0, rows, :].astype(F32)
            w_lo, a_lo, g_lo = jnp.tanh(sm[:, :2 * RW_RANK]), sm[:, 2 * RW_RANK:4 * RW_RANK], sm[:, 4 * RW_RANK:6 * RW_RANK]
            lw = -RW_DECAY_SCALE * jax.nn.sigmoid(low_rank(w_lo, d, wup_ref, w0_ref))
            a = jax.nn.sigmoid(low_rank(a_lo, d, aup_ref, a0_ref))
            kd = k * (1.0 + (a - 1.0) * ka_ref[...])
            kk_raw = k * kk_ref[...]
            lw_hi = lw.astype(BF16)
            lw_lo = (lw - lw_hi.astype(F32)).astype(BF16)
            cum = jnp.dot(incl.astype(BF16), jnp.concatenate([lw_hi, lw_lo], axis=1), preferred_element_type=F32)
            lp_inc = cum[:, :BRANCH_W] + cum[:, BRANCH_W:]
            p_end = jnp.exp(lp_inc[last, :])
            e_inc = jnp.exp(lp_inc)
            e_inv = jnp.exp(-lp_inc)
            e_exc = jnp.exp(lp_inc - lw)
            if not rev:
                a_b = jax.nn.sigmoid(low_rank(a_lo, 1, aup_ref, a0_ref))
                kd_sum = kd + k * (1.0 + (a_b - 1.0) * ka_ref[...])
                gate_s[rows, :] = bdot(jax.nn.sigmoid(g_lo), gup_ref[...])
            for h in range(N_HEADS):
                hc = slice(h * HEAD_DIM, (h + 1) * HEAD_DIM)
                kk_h = kk_raw[:, hc]
                kk_h = kk_h / jnp.maximum(jnp.sqrt(jnp.sum(kk_h * kk_h, axis=1, keepdims=True)), 1e-12)
                streams.append(dict(
                    j=d * N_HEADS + h, d=d, rows=rows, hc=hc, strict=strict, incl=incl, p_end=p_end[:, hc],
                    b_hat=(kk_h * a[:, hc] * e_inv[:, hc]).astype(BF16), k_hat=(kd[:, hc] * e_inv[:, hc]).astype(BF16),
                    kk_t=kk_h * e_exc[:, hc], r_t=r[:, hc] * e_inc[:, hc], v_h=v[:, hc].astype(BF16)))
                if not rev:
                    bonus_s[rows, hc] = jnp.sum(r[:, hc] * kd_sum[:, hc] * rk_ref[:, hc], axis=1, keepdims=True) * v[:, hc]

        for s in streams:
            s["p4"] = _dot_nt(jnp.concatenate([s["kk_t"], s["r_t"]], axis=0).astype(BF16),
                              jnp.concatenate([s["b_hat"], s["k_hat"]], axis=0))
            s["pw"] = [jnp.where(s["strict"], -s["p4"][:CHUNK, :CHUNK], 0.0).astype(BF16)]
            s["pair"] = []
        for step in range(1, 6):
            for s in streams:
                s["pw"].append(jnp.dot(s["pw"][-1], s["pw"][-1], preferred_element_type=F32).astype(BF16))
            if step % 2 == 1:
                for s in streams:
                    lo, hi = s["pw"][step - 1], s["pw"][step]
                    s["pair"].append(ident + lo.astype(F32) + hi.astype(F32) + jnp.dot(lo, hi, preferred_element_type=F32))
        for s in streams:
            s["inv"] = bdot(s["pair"][0], s["pair"][1])
        for s in streams:
            s["inv"] = bdot(s["inv"], s["pair"][2])
        for s in streams:
            p4 = s["p4"]
            masked = jnp.concatenate([jnp.where(s["strict"], p4[:CHUNK, CHUNK:], 0.0),
                                      jnp.where(s["incl"], p4[CHUNK:, CHUNK:], 0.0)], axis=0)
            s["akv_rkv"] = bdot(masked, s["v_h"])
        for s in streams:
            s["m12"] = bdot(s["inv"], jnp.concatenate([s["kk_t"], s["akv_rkv"][:CHUNK]], axis=1)).astype(BF16)
        for s in streams:
            s["gh"] = _dot_tn(s["m12"], s["b_hat"])
            s["vk"] = _dot_tn(s["v_h"], s["k_hat"])
        for s in streams:
            s["qy"] = bdot(jnp.where(s["incl"], s["p4"][CHUNK:, :CHUNK], 0.0), s["m12"])
        for s in streams:
            g_mat = (ident - s["gh"][:HEAD_DIM]) * s["p_end"]
            h_mat = (s["vk"] - s["gh"][HEAD_DIM:]) * s["p_end"]
            q_mat = s["r_t"] - s["qy"][:, :HEAD_DIM]
            y0 = s["akv_rkv"][CHUNK:] - s["qy"][:, HEAD_DIM:]
            s0 = st_s[s["j"]].astype(BF16)
            y_s[s["d"], s["rows"], s["hc"]] = _dot_nt(q_mat.astype(BF16), s0) + y0
            st_s[s["j"]] = bdot(s0, g_mat) + h_mat
        return carry

    lax.fori_loop(0, nch // RW_CHUNKS_PER_STEP, chunk_step, 0)

    head_mean = _head_mean_matrix()

    def finish(i, carry):
        rows = [pl.ds(pl.multiple_of((RW_CHUNKS_PER_STEP * i + u) * CHUNK, CHUNK), CHUNK) for u in range(RW_CHUNKS_PER_STEP)]
        ys = [y_s[0, r, :] + y_s[1, r, :] for r in rows]
        ycs = [y - _split_dot(y, head_mean) for y in ys]
        vrs = [_split_dot(yc * yc, head_mean) for yc in ycs]
        for r, yc, vr in zip(rows, ycs, vrs):
            tot = yc * lax.rsqrt(vr + RW_NORM_EPS) * ng_ref[...] + nb_ref[...]
            out_ref[0, r, :] = ((tot + bonus_s[r, :]) * gate_s[r, :]).astype(BF16)
        return carry

    lax.fori_loop(0, nch // RW_CHUNKS_PER_STEP, finish, 0)


def _rwkv(rkv, small, conv_w, w0, w_up, a0, a_up, g_up, k_k, k_a, r_k, norm_g, norm_b):
    bsz, s, _ = rkv.shape
    vec = lambda t: t.astype(F32).reshape(1, BRANCH_W)
    f = lambda t: t.astype(F32)
    return pl.pallas_call(
        _rwkv_body,
        out_shape=jax.ShapeDtypeStruct((bsz, s, BRANCH_W), BF16),
        grid=(bsz,),
        in_specs=[pl.BlockSpec((1, s, W_RKV), lambda b: (b, 0, 0)),
                  pl.BlockSpec((1, s, W_SMALL), lambda b: (b, 0, 0)),
                  _resident((3, 3 * BRANCH_W)), _resident((2, BRANCH_W)), _resident((2, RW_RANK, BRANCH_W)),
                  _resident((2, BRANCH_W)), _resident((2, RW_RANK, BRANCH_W)), _resident((2 * RW_RANK, BRANCH_W)),
                  _resident((1, BRANCH_W)), _resident((1, BRANCH_W)), _resident((1, BRANCH_W)),
                  _resident((1, BRANCH_W)), _resident((1, BRANCH_W))],
        out_specs=pl.BlockSpec((1, s, BRANCH_W), lambda b: (b, 0, 0)),
        scratch_shapes=[pltpu.VMEM((2, s, BRANCH_W), F32), pltpu.VMEM((s, BRANCH_W), F32), pltpu.VMEM((s, BRANCH_W), F32),
                        pltpu.VMEM((N_STREAMS, HEAD_DIM, HEAD_DIM), F32)],
        compiler_params=_params(("parallel",)),
        name="rwkv7",
    )(rkv, small, f(conv_w), f(w0), w_up.astype(BF16), f(a0), a_up.astype(BF16), g_up.astype(BF16), vec(k_k), vec(k_a),
      vec(r_k), vec(norm_g), vec(norm_b))


def _trunk(x, p, w_in_perm):
    bsz, s, d = x.shape
    n = bsz * s
    seq = lambda t: t.reshape(bsz, s, t.shape[-1])
    flat = lambda t: t.reshape(n, t.shape[-1])
    xf = _ln_rows(x.reshape(n, d), p["ln0_g"], p["ln0_b"])
    for l in range(DEPTH):
        na, mlqkv, mlo, rkv, small, merge = _in_proj(xf, w_in_perm[l])
        o_a = _na_attention(seq(na), _na_bias_table(p["na_rpb"][l], s // GRID_W))
        o_b = _mlstm(seq(mlqkv), seq(mlo), seq(small), p["ml_conv"][l], p["ml_gate_b"][l], p["ml_norm_g"][l])
        o_c = _rwkv(seq(rkv), seq(small), p["rw_conv"][l], p["rw_w0"][l], p["rw_w_up"][l], p["rw_a0"][l], p["rw_a_up"][l],
                    p["rw_g_up"][l], p["rw_k_k"][l], p["rw_k_a"][l], p["rw_r_k"][l], p["rw_norm_g"][l], p["rw_norm_b"][l])
        xf = _mix_ln(xf, flat(o_a), flat(o_b), flat(o_c), merge, p["w_br_a"][l], p["w_br_b"][l], p["w_br_c"][l],
                     p["w_out"][l], p["ln1_g"][l], p["ln1_b"][l])
        xf = _moe_ln(xf, p["w_router"], p["router_bias"], p["moe_w1"][l], p["moe_w3"][l], p["moe_w2"][l],
                     p["ln2_g"][l], p["ln2_b"][l])
    return xf.reshape(bsz, s, d)


def kernel(x_prompt, x_sample, ln0_g, ln0_b, w_in, na_rpb, ml_conv, ml_gate_b, ml_norm_g, rw_conv, rw_w0, rw_w_up,
           rw_a0, rw_a_up, rw_g_up, rw_k_k, rw_k_a, rw_r_k, rw_norm_g, rw_norm_b, w_br_a, w_br_b, w_br_c, w_out,
           ln1_g, ln1_b, w_router, router_bias, moe_w1, moe_w3, moe_w2, ln2_g, ln2_b):
    p = {
        "ln0_g": ln0_g, "ln0_b": ln0_b, "na_rpb": na_rpb, "ml_conv": ml_conv,
        "ml_gate_b": ml_gate_b, "ml_norm_g": ml_norm_g, "rw_conv": rw_conv, "rw_w0": rw_w0,
        "rw_w_up": rw_w_up, "rw_a0": rw_a0, "rw_a_up": rw_a_up, "rw_g_up": rw_g_up, "rw_k_k": rw_k_k,
        "rw_k_a": rw_k_a, "rw_r_k": rw_r_k, "rw_norm_g": rw_norm_g, "rw_norm_b": rw_norm_b,
        "w_br_a": w_br_a, "w_br_b": w_br_b, "w_br_c": w_br_c, "w_out": w_out, "ln1_g": ln1_g,
        "ln1_b": ln1_b, "w_router": w_router, "router_bias": router_bias, "moe_w1": moe_w1,
        "moe_w3": moe_w3, "moe_w2": moe_w2, "ln2_g": ln2_g, "ln2_b": ln2_b,
    }
    w_in_perm = [_permute_w_in(w_in[l]) for l in range(DEPTH)]
    return _trunk(x_prompt, p, w_in_perm), _trunk(x_sample, p, w_in_perm)
```

```python
import functools

import jax
import jax.numpy as jnp
import numpy as np
from jax import lax
from jax.experimental import pallas as pl
from jax.experimental.pallas import tpu as pltpu

F32 = jnp.float32
BF16 = jnp.bfloat16

D_MODEL = 1024
DEPTH = 4
GRID_W = 64
HEAD_DIM = 64
N_HEADS = 4
BRANCH_W = N_HEADS * HEAD_DIM
CHUNK = 64
NA_WIN_ROWS = 8
NA_WIN_COLS = 16
NA_ROW_CLASSES = 8
ML_NORM_EPS = 1e-6
RW_RANK = 32
RW_DECAY_SCALE = 0.606531
RW_NORM_EPS = 64e-5
N_EXPERTS = 16
EXPERTS_PER_GROUP = 4
D_EXPERT = 512
ALPHA = (2 * DEPTH) ** 0.25
LN_EPS = 1e-5
NEG = -1e30

W_NA = 3 * BRANCH_W
W_MLQKV = 3 * BRANCH_W
W_MLO = BRANCH_W
W_RKV = 3 * BRANCH_W
W_SMALL = 256
W_MERGE = 3 * D_MODEL
PIECES = (W_NA, W_MLQKV, W_MLO, W_RKV, W_SMALL, W_MERGE)
D_INP = sum(PIECES)
SMALL_GATE_OFF = 192

VMEM_LIMIT = 56 * 1024 * 1024
TOKEN_TILE = 1024


def _params(sem):
    return pltpu.CompilerParams(dimension_semantics=sem, vmem_limit_bytes=VMEM_LIMIT)


def _resident(shape):
    nd = len(shape)
    return pl.BlockSpec(shape, lambda *_: (0,) * nd, pipeline_mode=pl.Buffered(1))


def _ln(z, g, b):
    mu = jnp.mean(z, axis=-1, keepdims=True)
    zc = z - mu
    var = jnp.mean(zc * zc, axis=-1, keepdims=True)
    return zc * lax.rsqrt(var + LN_EPS) * g + b


def _ln_rows_body(x_ref, g_ref, b_ref, o_ref):
    o_ref[...] = _ln(x_ref[...], g_ref[...], b_ref[...])


def _ln_rows(x, g, b):
    n, d = x.shape
    tm = TOKEN_TILE
    return pl.pallas_call(
        _ln_rows_body,
        out_shape=jax.ShapeDtypeStruct((n, d), F32),
        grid=(n // tm,),
        in_specs=[pl.BlockSpec((tm, d), lambda i: (i, 0)), _resident((1, d)), _resident((1, d))],
        out_specs=pl.BlockSpec((tm, d), lambda i: (i, 0)),
        compiler_params=_params(("parallel",)),
        name="ln_rows",
    )(x, g.reshape(1, d), b.reshape(1, d))


def _in_proj_body(x_ref, w_ref, *o_refs):
    xb = x_ref[...].astype(BF16)
    off = 0
    for o_ref, width in zip(o_refs, PIECES):
        for c0 in range(0, width, 768):
            c1 = min(c0 + 768, width)
            o_ref[:, c0:c1] = jnp.dot(xb, w_ref[:, off + c0:off + c1], preferred_element_type=F32).astype(BF16)
        off += width


def _in_proj(x, w):
    n, d = x.shape
    tm = TOKEN_TILE
    return pl.pallas_call(
        _in_proj_body,
        out_shape=[jax.ShapeDtypeStruct((n, wd), BF16) for wd in PIECES],
        grid=(n // tm,),
        in_specs=[pl.BlockSpec((tm, d), lambda i: (i, 0)), _resident((d, D_INP))],
        out_specs=[pl.BlockSpec((tm, wd), lambda i: (i, 0)) for wd in PIECES],
        compiler_params=_params(("parallel",)),
        name="in_proj",
    )(x, w)


def _permute_w_in(w_in_l):
    na, mlqkv, mlo, mlg, rkv, wl, al, gl, merge = jnp.split(
        w_in_l, np.cumsum([768, 768, 256, 16, 768, 64, 64, 64])[:8].tolist(), axis=1)
    pad = jnp.zeros((w_in_l.shape[0], W_SMALL - 208), w_in_l.dtype)
    return jnp.concatenate([na, mlqkv, mlo, rkv, wl, al, gl, mlg, pad, merge], axis=1).astype(BF16)


def _mix_ln_body(x_ref, oa_ref, ob_ref, oc_ref, mg_ref, wa_ref, wb_ref, wc_ref, wo_ref, g_ref, b_ref, o_ref):
    mixed = None
    for j, (o_br, w_br) in enumerate(((oa_ref, wa_ref), (ob_ref, wb_ref), (oc_ref, wc_ref))):
        gate = jax.nn.sigmoid(mg_ref[:, j * D_MODEL:(j + 1) * D_MODEL].astype(F32))
        term = gate * jnp.dot(o_br[...], w_br[...], preferred_element_type=F32)
        mixed = term if mixed is None else mixed + term
    z = ALPHA * x_ref[...] + jnp.dot(mixed.astype(BF16), wo_ref[...], preferred_element_type=F32)
    o_ref[...] = _ln(z, g_ref[...], b_ref[...])


def _mix_ln(x, oa, ob, oc, merge, wa, wb, wc, wo, g, b):
    n, d = x.shape
    tm = TOKEN_TILE
    row = lambda wd: pl.BlockSpec((tm, wd), lambda i: (i, 0))
    return pl.pallas_call(
        _mix_ln_body,
        out_shape=jax.ShapeDtypeStruct((n, d), F32),
        grid=(n // tm,),
        in_specs=[row(d), row(BRANCH_W), row(BRANCH_W), row(BRANCH_W), row(W_MERGE),
                  _resident((BRANCH_W, d)), _resident((BRANCH_W, d)), _resident((BRANCH_W, d)), _resident((d, d)),
                  _resident((1, d)), _resident((1, d))],
        out_specs=row(d),
        compiler_params=_params(("parallel",)),
        name="mix_ln",
    )(x, oa, ob, oc, merge, wa.astype(BF16), wb.astype(BF16), wc.astype(BF16), wo.astype(BF16),
      g.reshape(1, d), b.reshape(1, d))


MOE_TILE = 1024


def _top2_sum(a, b, c, d):
    hi1, lo1 = jnp.maximum(a, b), jnp.minimum(a, b)
    hi2, lo2 = jnp.maximum(c, d), jnp.minimum(c, d)
    return jnp.maximum(hi1, hi2) + jnp.maximum(jnp.minimum(hi1, hi2), jnp.maximum(lo1, lo2))


def _router_gates(x, wrt, rbias):
    w_hi, x_hi = wrt.astype(BF16), x.astype(BF16)
    w_lo, x_lo = (wrt - w_hi.astype(F32)).astype(BF16), (x - x_hi.astype(F32)).astype(BF16)
    logits = _dot_nt(w_hi, x_hi) + (_dot_nt(w_hi, x_lo) + _dot_nt(w_lo, x_hi))
    ex = jnp.exp(logits - jnp.max(logits, axis=0, keepdims=True))
    probs = ex / jnp.sum(ex, axis=0, keepdims=True)
    sel = probs + rbias
    rows = [sel[e:e + 1, :] for e in range(N_EXPERTS)]
    n_groups = N_EXPERTS // EXPERTS_PER_GROUP
    scores = [_top2_sum(*rows[EXPERTS_PER_GROUP * g:EXPERTS_PER_GROUP * (g + 1)]) for g in range(n_groups)]
    best, gidx = scores[0], jnp.zeros_like(scores[0], dtype=jnp.int32)
    for g in range(1, n_groups):
        better = scores[g] > best
        gidx = jnp.where(better, g, gidx)
        best = jnp.where(better, scores[g], best)
    picked = []
    for e in range(N_EXPERTS):
        g = e // EXPERTS_PER_GROUP
        rank = jnp.zeros_like(gidx)
        for e2 in range(EXPERTS_PER_GROUP * g, EXPERTS_PER_GROUP * (g + 1)):
            if e2 == e:
                continue
            ahead = (rows[e2] >= rows[e]) if e2 < e else (rows[e2] > rows[e])
            rank = rank + ahead.astype(jnp.int32)
        chosen = (gidx == g) & (rank < 2)
        picked.append(jnp.where(chosen, probs[e:e + 1, :], 0.0))
    total = picked[0]
    for p in picked[1:]:
        total = total + p
    slots = []
    for k in range(EXPERTS_PER_GROUP):
        acc = picked[k]
        for g in range(1, n_groups):
            acc = acc + picked[EXPERTS_PER_GROUP * g + k]
        slots.append(acc / total)
    return jnp.concatenate(slots, axis=0), gidx


def _moe_ln_body(x_ref, wrt_ref, rb_ref, tri_ref, w1_ref, w3_ref, w2_ref, g_ref, b_ref, o_ref,
                 xb_s, rank_s, gidx_s, slot_s, pick_s, ys_s, cnt_s):
    grp = pl.program_id(1)
    tm = x_ref.shape[0]
    n_groups = N_EXPERTS // EXPERTS_PER_GROUP

    @pl.when(grp == 0)
    def _():
        x = x_ref[...]
        xb_s[...] = x.astype(BF16)
        slots, gidx = _router_gates(x, wrt_ref[...], rb_ref[...])
        hi = slots.astype(BF16)
        slot_s[...] = jnp.concatenate([hi, (slots - hi.astype(F32)).astype(BF16)], axis=0)
        gidx_s[...] = gidx
        member = [jnp.where(gidx == g, 1.0, 0.0) for g in range(n_groups)]
        onehot = jnp.concatenate(member + [jnp.zeros((8 - n_groups, tm), F32)], axis=0)
        upto = jnp.dot(onehot.astype(BF16), tri_ref[...], preferred_element_type=F32)
        rank = -1.0
        for g in range(n_groups):
            rank = rank + member[g] * upto[g:g + 1, :]
            cnt_s[g] = jnp.sum(member[g]).astype(jnp.int32)
        rank_s[...] = rank
        o_ref[...] = jnp.zeros_like(o_ref)

    n_blocks = (cnt_s[grp] + MOE_BLOCK - 1) // MOE_BLOCK

    def block(blk):
        want = (lax.broadcasted_iota(jnp.int32, (MOE_BLOCK, tm), 0) + blk * MOE_BLOCK).astype(F32)
        pick = jnp.where((rank_s[...] == want) & (gidx_s[...] == grp), 1.0, 0.0).astype(BF16)
        xs = jnp.dot(pick, xb_s[...], preferred_element_type=F32).astype(BF16)
        gates = _dot_nt(pick, slot_s[...])
        gates = gates[:, :EXPERTS_PER_GROUP] + gates[:, EXPERTS_PER_GROUP:]
        ys = None
        for pair in range(0, EXPERTS_PER_GROUP, 2):
            ks = (pair, pair + 1)
            h1s = [jnp.dot(xs, w1_ref[k], preferred_element_type=F32) for k in ks]
            h3s = [jnp.dot(xs, w3_ref[k], preferred_element_type=F32) for k in ks]
            hs = [(h1 * jax.nn.sigmoid(h1) * h3 * gates[:, k:k + 1]).astype(BF16) for k, h1, h3 in zip(ks, h1s, h3s)]
            for k, h in zip(ks, hs):
                y = jnp.dot(h, w2_ref[k], preferred_element_type=F32)
                ys = y if ys is None else ys + y
        return pick, ys.astype(BF16)

    first = pl.ds(pl.multiple_of(grp * MOE_BLOCK, MOE_BLOCK), MOE_BLOCK)
    pick_s[first, :], ys_s[first, :] = block(0)

    def extra(blk, carry):
        pick, ys = block(blk)
        o_ref[...] += _dot_tn(pick, ys)
        return carry

    lax.fori_loop(1, n_blocks, extra, 0)

    @pl.when(grp == n_groups - 1)
    def _():
        y = o_ref[...] + _dot_tn(pick_s[...], ys_s[...])
        o_ref[...] = _ln(ALPHA * x_ref[...] + y, g_ref[...], b_ref[...])


MOE_BLOCK = 304


def _moe_ln(x, w_router, router_bias, w1, w3, w2, g, b):
    n, d = x.shape
    tm = MOE_TILE
    tri = (np.arange(tm)[:, None] <= np.arange(tm)[None, :])
    grp_w = lambda shape: pl.BlockSpec((EXPERTS_PER_GROUP,) + shape, lambda i, g: (g, 0, 0))
    return pl.pallas_call(
        _moe_ln_body,
        out_shape=jax.ShapeDtypeStruct((n, d), F32),
        grid=(n // tm, N_EXPERTS // EXPERTS_PER_GROUP),
        in_specs=[pl.BlockSpec((tm, d), lambda i, g: (i, 0)),
                  _resident((N_EXPERTS, d)), _resident((N_EXPERTS, 1)), _resident((tm, tm)),
                  grp_w((d, D_EXPERT)), grp_w((d, D_EXPERT)), grp_w((D_EXPERT, d)),
                  _resident((1, d)), _resident((1, d))],
        out_specs=pl.BlockSpec((tm, d), lambda i, g: (i, 0)),
        scratch_shapes=[pltpu.VMEM((tm, d), BF16), pltpu.VMEM((1, tm), F32), pltpu.VMEM((1, tm), jnp.int32),
                        pltpu.VMEM((2 * EXPERTS_PER_GROUP, tm), BF16),
                        pltpu.VMEM((N_EXPERTS // EXPERTS_PER_GROUP * MOE_BLOCK, tm), BF16),
                        pltpu.VMEM((N_EXPERTS // EXPERTS_PER_GROUP * MOE_BLOCK, d), BF16),
                        pltpu.SMEM((N_EXPERTS // EXPERTS_PER_GROUP,), jnp.int32)],
        compiler_params=_params(("parallel", "arbitrary")),
        name="moe_ln",
    )(x, w_router.T, router_bias.reshape(N_EXPERTS, 1), jnp.asarray(tri, BF16), w1.astype(BF16), w3.astype(BF16),
      w2.astype(BF16), g.reshape(1, d), b.reshape(1, d))


def _na_bias_table(rpb, rows):
    assert rows >= NA_WIN_ROWS
    cols = np.arange(GRID_W)
    win = np.clip(cols - NA_WIN_COLS // 2, 0, GRID_W - NA_WIN_COLS)
    rel = cols[None, :] - cols[:, None]
    ok = (cols[None, :] >= win[:, None]) & (cols[None, :] < win[:, None] + NA_WIN_COLS)
    pick = (np.arange(2 * NA_WIN_COLS - 1)[None, None, :] == (rel + NA_WIN_COLS - 1)[:, :, None]) & ok[:, :, None]
    t = jnp.einsum("hdr,ckr->hdck", rpb.astype(F32), jnp.asarray(pick, F32), precision=lax.Precision.HIGHEST)
    t = jnp.where(ok[None, None], t, NEG)
    per_class = [t[:, NA_WIN_ROWS - 1 - cls:2 * NA_WIN_ROWS - 1 - cls] for cls in range(NA_ROW_CLASSES)]
    t = jnp.stack(per_class, axis=1)
    return jnp.transpose(t, (0, 1, 3, 2, 4)).reshape(rpb.shape[0], NA_ROW_CLASSES, GRID_W, NA_WIN_ROWS * GRID_W)


NA_ROWS_PER_STEP = 2


def _na_body(qkv_ref, bias_ref, o_ref, vx_s):
    rows = qkv_ref.shape[1] // GRID_W
    band = NA_WIN_ROWS * GRID_W

    head_cols = lambda base, h: slice(base + h * HEAD_DIM, base + (h + 1) * HEAD_DIM)
    ones_col = jnp.where(lax.broadcasted_iota(jnp.int32, (GRID_W, HEAD_DIM), 1) == 0, 1.0, 0.0).astype(BF16)

    def extend(r, carry):
        q_rows = pl.ds(pl.multiple_of(r * GRID_W, GRID_W), GRID_W)
        for h in range(N_HEADS):
            vx_s[q_rows, 2 * h * HEAD_DIM:(2 * h + 1) * HEAD_DIM] = qkv_ref[0, q_rows, head_cols(2 * BRANCH_W, h)]
            vx_s[q_rows, (2 * h + 1) * HEAD_DIM:(2 * h + 2) * HEAD_DIM] = ones_col
        return carry

    lax.fori_loop(0, rows, extend, 0)

    def row_step(i, carry):
        units = []
        for u in range(NA_ROWS_PER_STEP):
            r = NA_ROWS_PER_STEP * i + u
            rs = jnp.clip(r - NA_WIN_ROWS // 2, 0, rows - NA_WIN_ROWS)
            cls = jnp.where(r < NA_WIN_ROWS // 2, r, NA_WIN_ROWS // 2 + jnp.maximum(r - (rows - NA_WIN_ROWS // 2), 0))
            q_rows = pl.ds(pl.multiple_of(r * GRID_W, GRID_W), GRID_W)
            k_rows = pl.ds(pl.multiple_of(rs * GRID_W, GRID_W), band)
            units += [(h, cls, q_rows, k_rows) for h in range(N_HEADS)]
        scores = [_dot_nt(qkv_ref[0, q_rows, head_cols(0, h)], qkv_ref[0, k_rows, head_cols(BRANCH_W, h)])
                  for h, _, q_rows, k_rows in units]
        maxes = [jnp.max(scores[n] * (HEAD_DIM ** -0.5) + bias_ref[h, cls], axis=1, keepdims=True)
                 for n, (h, cls, _, _) in enumerate(units)]
        probs = [jnp.exp(scores[n] * (HEAD_DIM ** -0.5) + bias_ref[h, cls] - maxes[n]) for n, (h, cls, _, _) in enumerate(units)]
        outs = [jnp.dot(probs[n].astype(BF16), vx_s[k_rows, 2 * h * HEAD_DIM:(2 * h + 2) * HEAD_DIM], preferred_element_type=F32)
                for n, (h, _, _, k_rows) in enumerate(units)]
        for n, (h, _, q_rows, _) in enumerate(units):
            o_ref[0, q_rows, head_cols(0, h)] = (outs[n][:, :HEAD_DIM] / outs[n][:, HEAD_DIM:HEAD_DIM + 1]).astype(BF16)
        return carry

    lax.fori_loop(0, rows // NA_ROWS_PER_STEP, row_step, 0)


def _na_attention(qkv, bias):
    bsz, s, _ = qkv.shape
    return pl.pallas_call(
        _na_body,
        out_shape=jax.ShapeDtypeStruct((bsz, s, BRANCH_W), BF16),
        grid=(bsz,),
        in_specs=[pl.BlockSpec((1, s, W_NA), lambda b: (b, 0, 0)), _resident(bias.shape)],
        out_specs=pl.BlockSpec((1, s, BRANCH_W), lambda b: (b, 0, 0)),
        scratch_shapes=[pltpu.VMEM((s, 2 * BRANCH_W), BF16)],
        compiler_params=_params(("parallel",)),
        name="na_attention",
    )(qkv, bias)


def _eye(n):
    return lax.broadcasted_iota(jnp.int32, (n, n), 0) == lax.broadcasted_iota(jnp.int32, (n, n), 1)


def _row_to_col(row, eye):
    return jnp.sum(jnp.where(eye, row, 0.0), axis=1, keepdims=True)


def _conv3_rows(ref, c, nch, cols, w_ref):
    seq = nch * CHUNK
    r0 = pl.multiple_of(c * CHUNK, CHUNK)
    x = ref[0, pl.ds(r0, CHUNK), cols].astype(F32)
    lo = pl.multiple_of(jnp.maximum(r0 - 16, 0), 16)
    hi = pl.multiple_of(jnp.minimum(r0 + CHUNK, seq - 16), 16)
    prev = ref[0, pl.ds(lo, 16), cols][15:16, :].astype(F32) * jnp.where(c > 0, 1.0, 0.0)
    nxt = ref[0, pl.ds(hi, 16), cols][0:1, :].astype(F32) * jnp.where(c < nch - 1, 1.0, 0.0)
    rid = lax.broadcasted_iota(jnp.int32, x.shape, 0)
    xm = jnp.where(rid == 0, prev, pltpu.roll(x, 1, axis=0))
    xp = jnp.where(rid == CHUNK - 1, nxt, pltpu.roll(x, CHUNK - 1, axis=0))
    return w_ref[0:1, :] * xm + w_ref[1:2, :] * x + w_ref[2:3, :] * xp


def _head_norm(h, eps):
    mu = jnp.mean(h, axis=-1, keepdims=True)
    hc = h - mu
    return hc * lax.rsqrt(jnp.mean(hc * hc, axis=-1, keepdims=True) + eps)


def _head_mean_matrix():
    hi = lax.broadcasted_iota(jnp.int32, (BRANCH_W, BRANCH_W), 0) // HEAD_DIM
    hj = lax.broadcasted_iota(jnp.int32, (BRANCH_W, BRANCH_W), 1) // HEAD_DIM
    return jnp.where(hi == hj, 1.0 / HEAD_DIM, 0.0).astype(BF16)


def _split_dot(x, m):
    hi = x.astype(BF16)
    lo = (x - hi.astype(F32)).astype(BF16)
    return jnp.dot(hi, m, preferred_element_type=F32) + jnp.dot(lo, m, preferred_element_type=F32)


def _head_norm_wide(y, head_mean, eps):
    yc = y - _split_dot(y, head_mean)
    return yc * lax.rsqrt(_split_dot(yc * yc, head_mean) + eps)


def _dot_nt(a, b, **kw):
    return lax.dot_general(a, b, (((1,), (1,)), ((), ())), preferred_element_type=F32, **kw)


def _dot_tn(a, b, **kw):
    return lax.dot_general(a, b, (((0,), (0,)), ((), ())), preferred_element_type=F32, **kw)


N_STREAMS = 2 * N_HEADS


ML_CHUNKS_PER_STEP = 4
STATE_W = 2 * HEAD_DIM


def _mlstm_body(qkv_ref, o_ref_in, g_ref, gb_ref, cw_ref, ng_ref, out_ref,
                qk_s, vx_s, ig_s, b_s, ct_s, mc_s, bl_s, mprev_s, cst_s, mst_s):
    seq = qkv_ref.shape[1]
    nch = seq // CHUNK
    eye = _eye(CHUNK)
    ri = lax.broadcasted_iota(jnp.int32, (CHUNK, CHUNK), 0)
    ci = lax.broadcasted_iota(jnp.int32, (CHUNK, CHUNK), 1)
    head_cols = lambda base, h: slice(base + h * HEAD_DIM, base + (h + 1) * HEAD_DIM)
    head_mean = _head_mean_matrix()
    ones_col =jnp.where(lax.broadcasted_iota(jnp.int32, (CHUNK, HEAD_DIM), 1) == 0, 1.0, 0.0).astype(BF16)

    def conv_step(c, carry):
        rows = pl.ds(pl.multiple_of(c * CHUNK, CHUNK), CHUNK)
        y = _conv3_rows(qkv_ref, c, nch, slice(0, 2 * BRANCH_W), cw_ref)
        y = y * jax.nn.sigmoid(y)
        scale = jnp.where(lax.broadcasted_iota(jnp.int32, (1, 2 * BRANCH_W), 1) < BRANCH_W, 1.0, HEAD_DIM ** -0.5)
        qk_s[rows, :] = (y * scale).astype(BF16)
        for h in range(N_HEADS):
            vx_s[rows, h * STATE_W:h * STATE_W + HEAD_DIM] = qkv_ref[0, rows, head_cols(2 * BRANCH_W, h)]
            vx_s[rows, h * STATE_W + HEAD_DIM:(h + 1) * STATE_W] = ones_col
        return carry

    lax.fori_loop(0, nch, conv_step, 0)

    for j in range(N_STREAMS):
        rev, h = j >= N_HEADS, j % N_HEADS
        gi = (2 * N_HEADS if rev else 0) + h
        ig_s[j] = g_ref[0, gi] + gb_ref[gi]
        fpre = g_ref[0, gi + N_HEADS] + gb_ref[gi + N_HEADS]
        lf = jnp.minimum(fpre, 0.0) - jnp.log1p(jnp.exp(-jnp.abs(fpre)))
        cum = (ri >= ci) if rev else (ri <= ci)
        b_s[j] = jnp.dot(lf, cum.astype(F32), precision=lax.Precision.HIGHEST, preferred_element_type=F32)
        cst_s[j] = jnp.zeros((HEAD_DIM, STATE_W), F32)
        mst_s[j] = jnp.full((1, 128), NEG, F32)

    def chunk_stats(i, carry):
        chunks = [ML_CHUNKS_PER_STEP * i + u for u in range(ML_CHUNKS_PER_STEP)]
        rows = [pl.ds(pl.multiple_of(c * CHUNK, CHUNK), CHUNK) for c in chunks]
        units = [(chunks[u], rows[u], j) for u in range(ML_CHUNKS_PER_STEP) for j in range(N_STREAMS)]
        a_rs, mcs = [], []
        for c, _, j in units:
            b_r = b_s[j, pl.ds(c, 1), :]
            bl = b_r[:, 0:1] if j >= N_HEADS else b_r[:, CHUNK - 1:CHUNK]
            a_rs.append(bl - b_r + ig_s[j, pl.ds(c, 1), :])
            bl_s[j, c] = jnp.broadcast_to(bl, (1, 128))
        for n, (c, _, j) in enumerate(units):
            mcs.append(jnp.max(a_rs[n], axis=1, keepdims=True))
            mc_s[j, c] = jnp.broadcast_to(mcs[n], (1, 128))
        diags = [jnp.where(eye, jnp.exp(a_rs[n] - mcs[n]), 0.0).astype(BF16) for n in range(len(units))]
        wvs = [jnp.dot(diags[n], vx_s[r, (j % N_HEADS) * STATE_W:(j % N_HEADS + 1) * STATE_W],
                       preferred_element_type=F32).astype(BF16) for n, (_, r, j) in enumerate(units)]
        for n, (c, r, j) in enumerate(units):
            ct_s[j, c] = _dot_tn(qk_s[r, head_cols(BRANCH_W, j % N_HEADS)], wvs[n])
        return carry

    lax.fori_loop(0, nch // ML_CHUNKS_PER_STEP, chunk_stats, 0)

    def scan_step(i, carry):
        for j in range(N_STREAMS):
            c = (nch - 1 - i) if j >= N_HEADS else i
            ct, m = cst_s[j], mst_s[j]
            bl, mc = bl_s[j, c], mc_s[j, c]
            m_new = jnp.maximum(bl + m, mc)
            cst_s[j] = jnp.exp(bl + m - m_new) * ct + jnp.exp(mc - m_new) * ct_s[j, c]
            mst_s[j] = m_new
            ct_s[j, c] = ct
            mprev_s[j, c] = m
        return carry

    lax.fori_loop(0, nch, scan_step, 0)

    sub = lax.broadcasted_iota(jnp.int32, (16, CHUNK), 0)
    sub_w = lax.broadcasted_iota(jnp.int32, (16, 2 * CHUNK), 0)
    lane_w = lax.broadcasted_iota(jnp.int32, (16, 2 * CHUNK), 1)
    first_lane = jnp.where(lax.broadcasted_iota(jnp.int32, (1, CHUNK), 1) == 0, 1.0, 0.0)
    rhs_fixed = jnp.where((sub_w < 2) & (lane_w < CHUNK + 2), 1.0,
                          jnp.where((sub_w >= 4) & (sub_w < 6) & (lane_w == CHUNK + 1), -1.0, 0.0))
    pad = jnp.full((1, CHUNK), NEG, F32)

    def hi_lo(x):
        hi = x.astype(BF16).astype(F32)
        return hi, x - hi

    def chunk_out(i, carry):
        chunks = [ML_CHUNKS_PER_STEP * i + u for u in range(ML_CHUNKS_PER_STEP)]
        rows = [pl.ds(pl.multiple_of(c * CHUNK, CHUNK), CHUNK) for c in chunks]
        units = [(u, j) for u in range(ML_CHUNKS_PER_STEP) for j in range(N_STREAMS)]
        qs = [[qk_s[r, head_cols(0, h)] for h in range(N_HEADS)] for r in rows]
        vs = [[vx_s[r, h * STATE_W:(h + 1) * STATE_W] for h in range(N_HEADS)] for r in rows]
        qks = [[_dot_nt(qs[u][h], qk_s[r, head_cols(BRANCH_W, h)]) for h in range(N_HEADS)] for u, r in enumerate(rows)]
        inter_parts = [jnp.dot(qs[u][j % N_HEADS], ct_s[j, chunks[u]].astype(BF16), preferred_element_type=F32)
                       for u, j in units]
        b_rs = [b_s[j, pl.ds(chunks[u], 1), :] for u, j in units]
        d_rs = [ig_s[j, pl.ds(chunks[u], 1), :] - b_rs[n] for n, (u, j) in enumerate(units)]
        m_prevs = [mprev_s[j, chunks[u]][:, 0:1] for u, j in units]
        run = jnp.concatenate([jnp.concatenate([d, pad], axis=1) for d in d_rs], axis=0)
        fwd_rows = lax.broadcasted_iota(jnp.int32, run.shape, 0) % N_STREAMS < N_HEADS
        shift = 1
        while shift < CHUNK:
            moved = jnp.where(fwd_rows, pltpu.roll(run, shift, axis=1), pltpu.roll(run, 2 * CHUNK - shift, axis=1))
            run = jnp.maximum(run, moved)
            shift *= 2
        exps = []
        for n in range(len(units)):
            u_hi, u_lo = hi_lo(-jnp.maximum(m_prevs[n], run[n:n + 1, :CHUNK]))
            b_hi, b_lo = hi_lo(b_rs[n])
            mid_hi, mid_lo = hi_lo(jnp.concatenate([d_rs[n], m_prevs[n] * first_lane], axis=1))
            lhs = jnp.where(sub == 0, u_hi, jnp.where(sub == 1, u_lo, jnp.where((sub == 2) | (sub == 3), 1.0,
                            jnp.where(sub == 4, b_hi, jnp.where(sub == 5, b_lo, 0.0)))))
            rhs = jnp.where(sub_w == 2, mid_hi, jnp.where(sub_w == 3, mid_lo, rhs_fixed))
            exps.append(jnp.exp(_dot_tn(lhs.astype(BF16), rhs.astype(BF16))))
        w_intras = []
        for n, (u, j) in enumerate(units):
            tri = (ci >= ri) if j >= N_HEADS else (ci <= ri)
            w_intras.append((jnp.where(tri, exps[n][:, :CHUNK], 0.0) * qks[u][j % N_HEADS]).astype(BF16))
        intra_parts = [jnp.dot(w_intras[n], vs[u][j % N_HEADS], preferred_element_type=F32) for n, (u, j) in enumerate(units)]
        hs = []
        for n in range(len(units)):
            inter, clamp = exps[n][:, CHUNK:CHUNK + 1], exps[n][:, CHUNK + 1:CHUNK + 2]
            both = inter * inter_parts[n] + intra_parts[n]
            den = both[:, HEAD_DIM:HEAD_DIM + 1]
            hs.append(both[:, :HEAD_DIM] / jnp.maximum(jnp.abs(den), clamp))
        tots = [jnp.concatenate([hs[u * N_STREAMS + h] + hs[u * N_STREAMS + N_HEADS + h] for h in range(N_HEADS)], axis=1)
                for u in range(ML_CHUNKS_PER_STEP)]
        centred = [t - _split_dot(t, head_mean) for t in tots]
        variances = [_split_dot(t * t, head_mean) for t in centred]
        for u in range(ML_CHUNKS_PER_STEP):
            tot = centred[u] * lax.rsqrt(variances[u] + ML_NORM_EPS) * ng_ref[...]
            out_ref[0, rows[u], :] = (tot * jax.nn.sigmoid(o_ref_in[0, rows[u], :].astype(F32))).astype(BF16)
        return carry

    lax.fori_loop(0, nch // ML_CHUNKS_PER_STEP, chunk_out, 0)


def _mlstm(qkv, o_pre, small, conv_w, gate_b, norm_g):
    bsz, s, _ = qkv.shape
    nch = s // CHUNK
    n_gate = 4 * N_HEADS
    gates = small[..., SMALL_GATE_OFF:SMALL_GATE_OFF + n_gate].astype(F32)
    gates = jnp.transpose(gates, (0, 2, 1)).reshape(bsz, n_gate, nch, CHUNK)
    gate_b = jnp.broadcast_to(gate_b.astype(F32).reshape(n_gate, 1, 1), (n_gate, 1, CHUNK))
    row = lambda shape: pltpu.VMEM(shape, F32)
    return pl.pallas_call(
        _mlstm_body,
        out_shape=jax.ShapeDtypeStruct((bsz, s, BRANCH_W), BF16),
        grid=(bsz,),
        in_specs=[pl.BlockSpec((1, s, W_MLQKV), lambda b: (b, 0, 0)),
                  pl.BlockSpec((1, s, BRANCH_W), lambda b: (b, 0, 0)),
                  pl.BlockSpec((1, n_gate, nch, CHUNK), lambda b: (b, 0, 0, 0)),
                  _resident((n_gate, 1, CHUNK)), _resident((3, 2 * BRANCH_W)), _resident((1, BRANCH_W))],
        out_specs=pl.BlockSpec((1, s, BRANCH_W), lambda b: (b, 0, 0)),
        scratch_shapes=[pltpu.VMEM((s, 2 * BRANCH_W), BF16), pltpu.VMEM((s, N_HEADS * STATE_W), BF16),
                        row((N_STREAMS, nch, CHUNK)), row((N_STREAMS, nch, CHUNK)),
                        row((N_STREAMS, nch, HEAD_DIM, STATE_W)),
                        row((N_STREAMS, nch, 1, 128)), row((N_STREAMS, nch, 1, 128)), row((N_STREAMS, nch, 1, 128)),
                        row((N_STREAMS, HEAD_DIM, STATE_W)), row((N_STREAMS, 1, 128))],
        compiler_params=_params(("parallel",)),
        name="mlstm",
    )(qkv, o_pre, gates, gate_b, conv_w.astype(F32), norm_g.astype(F32).reshape(1, BRANCH_W))


RW_CHUNKS_PER_STEP = 4


def _rwkv_body(rkv_ref, sm_ref, cw_ref, w0_ref, wup_ref, a0_ref, aup_ref, gup_ref, kk_ref, ka_ref, rk_ref, ng_ref, nb_ref,
               out_ref, y_s, bonus_s, gate_s, st_s):
    seq = rkv_ref.shape[1]
    nch = seq // CHUNK
    ri = lax.broadcasted_iota(jnp.int32, (CHUNK, CHUNK), 0)
    ci = lax.broadcasted_iota(jnp.int32, (CHUNK, CHUNK), 1)
    ident = (ri == ci).astype(F32)

    bdot = lambda x, y: jnp.dot(x.astype(BF16), y.astype(BF16), preferred_element_type=F32)

    def low_rank(x, d, up_ref, bias_ref):
        return bias_ref[d:d + 1, :] + bdot(x[:, d * RW_RANK:(d + 1) * RW_RANK], up_ref[d])

    st_s[...] = jnp.zeros_like(st_s)

    def chunk_step(i, carry):
        streams = []
        for rev, u in [(rev, u) for rev in (False, True) for u in range(RW_CHUNKS_PER_STEP)]:
            d = 1 if rev else 0
            strict = (ci > ri) if rev else (ci < ri)
            incl = (ci >= ri) if rev else (ci <= ri)
            last = slice(0, 1) if rev else slice(CHUNK - 1, CHUNK)
            c = (nch - 1 - (RW_CHUNKS_PER_STEP * i + u)) if rev else (RW_CHUNKS_PER_STEP * i + u)
            rows = pl.ds(pl.multiple_of(c * CHUNK, CHUNK), CHUNK)
            rkv = _conv3_rows(rkv_ref, c, nch, slice(0, 3 * BRANCH_W), cw_ref)
            r, k, v = rkv[:, :BRANCH_W], rkv[:, BRANCH_W:2 * BRANCH_W], rkv[:, 2 * BRANCH_W:]
            sm = sm_ref[0, rows, :].astype(F32)
            w_lo, a_lo, g_lo = jnp.tanh(sm[:, :2 * RW_RANK]), sm[:, 2 * RW_RANK:4 * RW_RANK], sm[:, 4 * RW_RANK:6 * RW_RANK]
            lw = -RW_DECAY_SCALE * jax.nn.sigmoid(low_rank(w_lo, d, wup_ref, w0_ref))
            a = jax.nn.sigmoid(low_rank(a_lo, d, aup_ref, a0_ref))
            kd = k * (1.0 + (a - 1.0) * ka_ref[...])
            kk_raw = k * kk_ref[...]
            lw_hi = lw.astype(BF16)
            lw_lo = (lw - lw_hi.astype(F32)).astype(BF16)
            cum = jnp.dot(incl.astype(BF16), jnp.concatenate([lw_hi, lw_lo], axis=1), preferred_element_type=F32)
            lp_inc = cum[:, :BRANCH_W] + cum[:, BRANCH_W:]
            p_end = jnp.exp(lp_inc[last, :])
            e_inc = jnp.exp(lp_inc)
            e_inv = jnp.exp(-lp_inc)
            e_exc = jnp.exp(lp_inc - lw)
            if not rev:
                a_b = jax.nn.sigmoid(low_rank(a_lo, 1, aup_ref, a0_ref))
                kd_sum = kd + k * (1.0 + (a_b - 1.0) * ka_ref[...])
                gate_s[rows, :] = bdot(jax.nn.sigmoid(g_lo), gup_ref[...])
            for h in range(N_HEADS):
                hc = slice(h * HEAD_DIM, (h + 1) * HEAD_DIM)
                kk_h = kk_raw[:, hc]
                kk_h = kk_h / jnp.maximum(jnp.sqrt(jnp.sum(kk_h * kk_h, axis=1, keepdims=True)), 1e-12)
                streams.append(dict(
                    j=d * N_HEADS + h, d=d, rows=rows, hc=hc, strict=strict, incl=incl, p_end=p_end[:, hc],
                    b_hat=(kk_h * a[:, hc] * e_inv[:, hc]).astype(BF16), k_hat=(kd[:, hc] * e_inv[:, hc]).astype(BF16),
                    kk_t=kk_h * e_exc[:, hc], r_t=r[:, hc] * e_inc[:, hc], v_h=v[:, hc].astype(BF16)))
                if not rev:
                    bonus_s[rows, hc] = jnp.sum(r[:, hc] * kd_sum[:, hc] * rk_ref[:, hc], axis=1, keepdims=True) * v[:, hc]

        for s in streams:
            s["p4"] = _dot_nt(jnp.concatenate([s["kk_t"], s["r_t"]], axis=0).astype(BF16),
                              jnp.concatenate([s["b_hat"], s["k_hat"]], axis=0))
            s["pw"] = [jnp.where(s["strict"], -s["p4"][:CHUNK, :CHUNK], 0.0).astype(BF16)]
            s["pair"] = []
        for step in range(1, 6):
            for s in streams:
                s["pw"].append(jnp.dot(s["pw"][-1], s["pw"][-1], preferred_element_type=F32).astype(BF16))
            if step % 2 == 1:
                for s in streams:
                    lo, hi = s["pw"][step - 1], s["pw"][step]
                    s["pair"].append(ident + lo.astype(F32) + hi.astype(F32) + jnp.dot(lo, hi, preferred_element_type=F32))
        for s in streams:
            s["inv"] = bdot(s["pair"][0], s["pair"][1])
        for s in streams:
            s["inv"] = bdot(s["inv"], s["pair"][2])
        for s in streams:
            p4 = s["p4"]
            masked = jnp.concatenate([jnp.where(s["strict"], p4[:CHUNK, CHUNK:], 0.0),
                                      jnp.where(s["incl"], p4[CHUNK:, CHUNK:], 0.0)], axis=0)
            s["akv_rkv"] = bdot(masked, s["v_h"])
        for s in streams:
            s["m12"] = bdot(s["inv"], jnp.concatenate([s["kk_t"], s["akv_rkv"][:CHUNK]], axis=1)).astype(BF16)
        for s in streams:
            s["gh"] = _dot_tn(s["m12"], s["b_hat"])
            s["vk"] = _dot_tn(s["v_h"], s["k_hat"])
        for s in streams:
            s["qy"] = bdot(jnp.where(s["incl"], s["p4"][CHUNK:, :CHUNK], 0.0), s["m12"])
        for s in streams:
            g_mat = (ident - s["gh"][:HEAD_DIM]) * s["p_end"]
            h_mat = (s["vk"] - s["gh"][HEAD_DIM:]) * s["p_end"]
            q_mat = s["r_t"] - s["qy"][:, :HEAD_DIM]
            y0 = s["akv_rkv"][CHUNK:] - s["qy"][:, HEAD_DIM:]
            s0 = st_s[s["j"]].astype(BF16)
            y_s[s["d"], s["rows"], s["hc"]] = _dot_nt(q_mat.astype(BF16), s0) + y0
            st_s[s["j"]] = bdot(s0, g_mat) + h_mat
        return carry

    lax.fori_loop(0, nch // RW_CHUNKS_PER_STEP, chunk_step, 0)

    head_mean = _head_mean_matrix()

    def finish(i, carry):
        rows = [pl.ds(pl.multiple_of((RW_CHUNKS_PER_STEP * i + u) * CHUNK, CHUNK), CHUNK) for u in range(RW_CHUNKS_PER_STEP)]
        ys = [y_s[0, r, :] + y_s[1, r, :] for r in rows]
        ycs = [y - _split_dot(y, head_mean) for y in ys]
        vrs = [_split_dot(yc * yc, head_mean) for yc in ycs]
        for r, yc, vr in zip(rows, ycs, vrs):
            tot = yc * lax.rsqrt(vr + RW_NORM_EPS) * ng_ref[...] + nb_ref[...]
            out_ref[0, r, :] = ((tot + bonus_s[r, :]) * gate_s[r, :]).astype(BF16)
        return carry

    lax.fori_loop(0, nch // RW_CHUNKS_PER_STEP, finish, 0)


def _rwkv(rkv, small, conv_w, w0, w_up, a0, a_up, g_up, k_k, k_a, r_k, norm_g, norm_b):
    bsz, s, _ = rkv.shape
    vec = lambda t: t.astype(F32).reshape(1, BRANCH_W)
    f = lambda t: t.astype(F32)
    return pl.pallas_call(
        _rwkv_body,
        out_shape=jax.ShapeDtypeStruct((bsz, s, BRANCH_W), BF16),
        grid=(bsz,),
        in_specs=[pl.BlockSpec((1, s, W_RKV), lambda b: (b, 0, 0)),
                  pl.BlockSpec((1, s, W_SMALL), lambda b: (b, 0, 0)),
                  _resident((3, 3 * BRANCH_W)), _resident((2, BRANCH_W)), _resident((2, RW_RANK, BRANCH_W)),
                  _resident((2, BRANCH_W)), _resident((2, RW_RANK, BRANCH_W)), _resident((2 * RW_RANK, BRANCH_W)),
                  _resident((1, BRANCH_W)), _resident((1, BRANCH_W)), _resident((1, BRANCH_W)),
                  _resident((1, BRANCH_W)), _resident((1, BRANCH_W))],
        out_specs=pl.BlockSpec((1, s, BRANCH_W), lambda b: (b, 0, 0)),
        scratch_shapes=[pltpu.VMEM((2, s, BRANCH_W), F32), pltpu.VMEM((s, BRANCH_W), F32), pltpu.VMEM((s, BRANCH_W), F32),
                        pltpu.VMEM((N_STREAMS, HEAD_DIM, HEAD_DIM), F32)],
        compiler_params=_params(("parallel",)),
        name="rwkv7",
    )(rkv, small, f(conv_w), f(w0), w_up.astype(BF16), f(a0), a_up.astype(BF16), g_up.astype(BF16), vec(k_k), vec(k_a),
      vec(r_k), vec(norm_g), vec(norm_b))


def _trunk(x, p, w_in_perm):
    bsz, s, d = x.shape
    n = bsz * s
    seq = lambda t: t.reshape(bsz, s, t.shape[-1])
    flat = lambda t: t.reshape(n, t.shape[-1])
    xf = _ln_rows(x.reshape(n, d), p["ln0_g"], p["ln0_b"])
    for l in range(DEPTH):
        na, mlqkv, mlo, rkv, small, merge = _in_proj(xf, w_in_perm[l])
        o_a = _na_attention(seq(na), _na_bias_table(p["na_rpb"][l], s // GRID_W))
        o_b = _mlstm(seq(mlqkv), seq(mlo), seq(small), p["ml_conv"][l], p["ml_gate_b"][l], p["ml_norm_g"][l])
        o_c = _rwkv(seq(rkv), seq(small), p["rw_conv"][l], p["rw_w0"][l], p["rw_w_up"][l], p["rw_a0"][l], p["rw_a_up"][l],
                    p["rw_g_up"][l], p["rw_k_k"][l], p["rw_k_a"][l], p["rw_r_k"][l], p["rw_norm_g"][l], p["rw_norm_b"][l])
        xf = _mix_ln(xf, flat(o_a), flat(o_b), flat(o_c), merge, p["w_br_a"][l], p["w_br_b"][l], p["w_br_c"][l],
                     p["w_out"][l], p["ln1_g"][l], p["ln1_b"][l])
        xf = _moe_ln(xf, p["w_router"], p["router_bias"], p["moe_w1"][l], p["moe_w3"][l], p["moe_w2"][l],
                     p["ln2_g"][l], p["ln2_b"][l])
    return xf.reshape(bsz, s, d)


def kernel(x_prompt, x_sample, ln0_g, ln0_b, w_in, na_rpb, ml_conv, ml_gate_b, ml_norm_g, rw_conv, rw_w0, rw_w_up,
           rw_a0, rw_a_up, rw_g_up, rw_k_k, rw_k_a, rw_r_k, rw_norm_g, rw_norm_b, w_br_a, w_br_b, w_br_c, w_out,
           ln1_g, ln1_b, w_router, router_bias, moe_w1, moe_w3, moe_w2, ln2_g, ln2_b):
    p = {
        "ln0_g": ln0_g, "ln0_b": ln0_b, "na_rpb": na_rpb, "ml_conv": ml_conv,
        "ml_gate_b": ml_gate_b, "ml_norm_g": ml_norm_g, "rw_conv": rw_conv, "rw_w0": rw_w0,
        "rw_w_up": rw_w_up, "rw_a0": rw_a0, "rw_a_up": rw_a_up, "rw_g_up": rw_g_up, "rw_k_k": rw_k_k,
        "rw_k_a": rw_k_a, "rw_r_k": rw_r_k, "rw_norm_g": rw_norm_g, "rw_norm_b": rw_norm_b,
        "w_br_a": w_br_a, "w_br_b": w_br_b, "w_br_c": w_br_c, "w_out": w_out, "ln1_g": ln1_g,
        "ln1_b": ln1_b, "w_router": w_router, "router_bias": router_bias, "moe_w1": moe_w1,
        "moe_w3": moe_w3, "moe_w2": moe_w2, "ln2_g": ln2_g, "ln2_b": ln2_b,
    }
    w_in_perm = [_permute_w_in(w_in[l]) for l in range(DEPTH)]
    return _trunk(x_prompt, p, w_in_perm), _trunk(x_sample, p, w_in_perm)
```
